```python
import math
import jax
import jax.numpy as jnp
from jax import lax
import numpy as np

D_MODEL = 2048
BATCH = 8
SEQ = 4096
DEPTH = 4

N_MIXERS = 3
PLE_DIM = 256
NORM_EPS = 1e-6
NEG_INF = -1e30
TINY = 1e-30
Q_BLOCK = 128

N_BUCKETS = 32
BUCKET_MAX_DIST = 2048
N_BIAS_HEADS = 16

LRU_WIDTH = D_MODEL
LRU_BLOCKS = 8
LRU_BLOCK_W = LRU_WIDTH // LRU_BLOCKS
CONV_W = 4
LRU_C = 8.0

DIL_PATTERNS = ((128, 1), (512, 4), (2048, 16))
N_DIL = len(DIL_PATTERNS)
DIL_HEADS = 16
DIL_HEAD_DIM = 64
DIL_WIDTH = DIL_HEADS * DIL_HEAD_DIM
DIL_IN_COLS = N_DIL * 3 * DIL_WIDTH + DIL_WIDTH

NSA_HEADS = 16
NSA_KV_GROUPS = 4
NSA_REP = NSA_HEADS // NSA_KV_GROUPS
NSA_HEAD_DIM = 128
NSA_WIDTH = NSA_HEADS * NSA_HEAD_DIM
NSA_KV_WIDTH = NSA_KV_GROUPS * NSA_HEAD_DIM
CMP_BLOCK = 32
CMP_STRIDE = 16
CMP_HIDDEN = 512
SEL_BLOCK = 64
SEL_TOPK = 16
WIN = 512
SEL_QCHUNK = 16
FORCE_BONUS = 1e4
NSA_SPLITS = (NSA_WIDTH,) + (NSA_KV_WIDTH,) * 6 + (3 * NSA_HEADS, NSA_WIDTH)
NSA_IN_COLS = sum(NSA_SPLITS)

kernel_name = "hybrid_rglru_dilated_nsa_trunk"


def _rmsnorm(x, g):
    xf = x.astype(jnp.float32)
    y = xf * lax.rsqrt(jnp.mean(xf * xf, axis=-1, keepdims=True) + NORM_EPS)
    return (y * g.astype(jnp.float32)).astype(x.dtype)


def _t5_bucket(dist):
    n = jnp.maximum(dist, 0)
    max_exact = N_BUCKETS // 2
    nf = jnp.maximum(n, max_exact).astype(jnp.float32)
    large = max_exact + (jnp.log(nf / max_exact) / math.log(BUCKET_MAX_DIST / max_exact)
                         * (N_BUCKETS - max_exact)).astype(jnp.int32)
    return jnp.where(n < max_exact, n, jnp.minimum(large, N_BUCKETS - 1))


def _masked_softmax(logits, mask):
    l = jnp.where(mask, logits, NEG_INF)
    m = jnp.max(l, axis=-1, keepdims=True)
    e = jnp.where(mask, jnp.exp(l - m), 0.0)
    s = jnp.sum(e, axis=-1, keepdims=True)
    return e / jnp.maximum(s, TINY), m[..., 0], s[..., 0]


def _unblock(t, length):
    nb, n, qb = t.shape[:3]
    t = jnp.moveaxis(t, 0, 1).reshape((n, nb * qb) + t.shape[3:])
    return t[:, :length]


def _banded_attention(q, k, v, rel_bias, lookback, max_delta, dilation):
    n, length, G, R, dh = q.shape
    nb = -(-length // Q_BLOCK)
    lp = nb * Q_BLOCK
    qp = jnp.pad(q, ((0, 0), (0, lp - length), (0, 0), (0, 0), (0, 0)))
    kp = jnp.pad(k, ((0, 0), (lookback, lp - length), (0, 0), (0, 0)))
    vp = jnp.pad(v, ((0, 0), (lookback, lp - length), (0, 0), (0, 0)))
    table = rel_bias.reshape(N_BUCKETS, G, R)
    span = Q_BLOCK + lookback
    scale = dh ** -0.5

    def block(b):
        q0 = b * Q_BLOCK
        qs = lax.dynamic_slice_in_dim(qp, q0, Q_BLOCK, axis=1)
        ks = lax.dynamic_slice_in_dim(kp, q0, span, axis=1)
        vs = lax.dynamic_slice_in_dim(vp, q0, span, axis=1)
        qpos = q0 + jnp.arange(Q_BLOCK)
        kpos = q0 - lookback + jnp.arange(span)
        delta = qpos[:, None] - kpos[None, :]
        mask = (delta >= 0) & (delta <= max_delta) & (kpos[None, :] >= 0)
        bias = jnp.transpose(table[_t5_bucket(delta * dilation)], (2, 3, 0, 1)).astype(jnp.float32)
        logits = jnp.einsum('nqgrd,nkgd->ngrqk', qs, ks).astype(jnp.float32) * scale + bias
        prob, m, s = _masked_softmax(logits, mask)
        o = jnp.einsum('ngrqk,nkgd->nqgrd', prob, vs.astype(jnp.float32))
        return o, jnp.transpose(m, (0, 3, 1, 2)), jnp.transpose(s, (0, 3, 1, 2))

    o, m, s = lax.map(block, jnp.arange(nb))
    return _unblock(o, length), _unblock(m, length), _unblock(s, length)


def _to_strided(t, d):
    bn, s = t.shape[:2]
    rest = t.shape[2:]
    t = jnp.swapaxes(t.reshape((bn, s // d, d) + rest), 1, 2)
    return t.reshape((bn * d, s // d) + rest)


def _from_strided(t, d, bn):
    n, length = t.shape[:2]
    rest = t.shape[2:]
    t = jnp.swapaxes(t.reshape((bn, d, length) + rest), 1, 2)
    return t.reshape((bn, length * d) + rest)


def _rglru_mixer(u, w_in, conv_w, conv_b, w_r, b_r, w_i, b_i, lam, w_out):
    bn, s, _ = u.shape
    proj = u @ w_in
    xb, gate = proj[..., :LRU_WIDTH], proj[..., LRU_WIDTH:]
    xc = lax.conv_general_dilated(xb, conv_w[:, None, :], window_strides=(1,),
                                  padding=((CONV_W - 1, 0),),
                                  dimension_numbers=('NWC', 'WIO', 'NWC'),
                                  feature_group_count=LRU_WIDTH) + conv_b
    xblk = xc.reshape(bn, s, LRU_BLOCKS, LRU_BLOCK_W)
    r = jax.nn.sigmoid(jnp.einsum('bsnc,ncd->bsnd', xblk, w_r) + b_r).reshape(bn, s, LRU_WIDTH)
    i = jax.nn.sigmoid(jnp.einsum('bsnc,ncd->bsnd', xblk, w_i) + b_i).reshape(bn, s, LRU_WIDTH)
    log_a = -LRU_C * r.astype(jnp.float32) * jax.nn.softplus(-lam.astype(jnp.float32))
    a = jnp.exp(log_a)
    bterm = jnp.sqrt(-jnp.expm1(2.0 * log_a)) * (i * xc).astype(jnp.float32)

    def combine(e1, e2):
        a1, b1 = e1
        a2, b2 = e2
        return a1 * a2, a2 * b1 + b2

    _, h = lax.associative_scan(combine, (a, bterm), axis=1)
    y = h.astype(u.dtype) * jax.nn.silu(gate)
    return y @ w_out


def _dilated_mixer(u, w_in, w_out, rel_bias):
    bn, s, _ = u.shape
    proj = u @ w_in
    qkv = proj[..., :N_DIL * 3 * DIL_WIDTH].reshape(bn, s, N_DIL, 3, DIL_HEADS, 1, DIL_HEAD_DIM)
    gate = proj[..., N_DIL * 3 * DIL_WIDTH:]
    outs, maxes, dens = [], [], []
    for gi, (window, dil) in enumerate(DIL_PATTERNS):
        q = _to_strided(qkv[:, :, gi, 0], dil)
        k = _to_strided(qkv[:, :, gi, 1, :, 0], dil)
        v = _to_strided(qkv[:, :, gi, 2, :, 0], dil)
        o, m, den = _banded_attention(q, k, v, rel_bias, window // dil, window // dil, dil)
        outs.append(_from_strided(o[:, :, :, 0], dil, bn))
        maxes.append(_from_strided(m[:, :, :, 0], dil, bn))
        dens.append(_from_strided(den[:, :, :, 0], dil, bn))
    o = jnp.stack(outs)
    m = jnp.stack(maxes)
    den = jnp.stack(dens)
    wts = den * jnp.exp(m - jnp.max(m, axis=0, keepdims=True))
    o = jnp.sum(wts[..., None] * o, axis=0) / jnp.sum(wts, axis=0)[..., None]
    y = o.reshape(bn, s, DIL_WIDTH).astype(u.dtype) * jax.nn.silu(gate)
    return y @ w_out


def _compress(t, cidx, pos, w1, w2):
    bn, nc = t.shape[0], cidx.shape[0]
    blk = t[:, cidx] + pos[None, None, :, None, :]
    blk = jnp.swapaxes(blk, 2, 3).reshape(bn, nc, NSA_KV_GROUPS, CMP_BLOCK * NSA_HEAD_DIM)
    return jax.nn.gelu(blk @ w1) @ w2


def _nsa_mixer(u, w_in, pos_k, w1_k, w2_k, pos_v, w1_v, w2_v, w_out, rel_bias):
    bn, s, _ = u.shape
    G, R, dh = NSA_KV_GROUPS, NSA_REP, NSA_HEAD_DIM
    scale = dh ** -0.5
    proj = u @ w_in
    offs = np.cumsum(NSA_SPLITS)[:-1].tolist()
    q, kc, vc, ksl, vsl, kw, vw, bgate, gpath = jnp.split(proj, offs, axis=-1)
    q = q.reshape(bn, s, G, R, dh)
    kc, vc, ksl, vsl, kw, vw = [t.reshape(bn, s, G, dh) for t in (kc, vc, ksl, vsl, kw, vw)]

    nc = (s - CMP_BLOCK) // CMP_STRIDE + 1
    cidx = np.arange(nc)[:, None] * CMP_STRIDE + np.arange(CMP_BLOCK)[None, :]
    cend = cidx[:, -1]
    kcmp = _compress(kc, cidx, pos_k, w1_k, w2_k)
    vcmp = _compress(vc, cidx, pos_v, w1_v, w2_v)
    nsel = s // SEL_BLOCK
    topk = min(SEL_TOPK, nsel)
    sel_start = np.arange(nsel) * SEL_BLOCK
    cover = jnp.asarray(((cidx[:, :1] < sel_start[None, :] + SEL_BLOCK)
                         & (cidx[:, -1:] >= sel_start[None, :])).astype(np.float32))
    blk_ids = jnp.arange(nsel)

    def cmp_block(b):
        q0 = b * Q_BLOCK
        qs = lax.dynamic_slice_in_dim(q, q0, Q_BLOCK, axis=1)
        t = q0 + jnp.arange(Q_BLOCK)
        mask = cend[None, :] <= t[:, None]
        logits = jnp.einsum('bqgrd,bcgd->bgrqc', qs, kcmp).astype(jnp.float32) * scale
        prob, _, _ = _masked_softmax(logits, mask)
        o = jnp.einsum('bgrqc,bcgd->bqgrd', prob, vcmp.astype(jnp.float32))
        imp = jnp.einsum('bgrqc,cn->bgqn', prob, cover)
        cur = t // SEL_BLOCK
        forced = ((blk_ids[None, :] == 0) | (blk_ids[None, :] == cur[:, None])
                  | (blk_ids[None, :] == cur[:, None] - 1)).astype(jnp.float32)
        valid = blk_ids[None, :] * SEL_BLOCK <= t[:, None]
        imp = jnp.where(valid, imp + FORCE_BONUS * forced, NEG_INF)
        _, idx = lax.top_k(imp, topk)
        return o, jnp.transpose(idx, (0, 2, 1, 3))

    o_cmp, sel_idx = lax.map(cmp_block, jnp.arange(s // Q_BLOCK))
    o_cmp = _unblock(o_cmp, s)
    sel_idx = _unblock(sel_idx, s)

    ksT = jnp.swapaxes(ksl, 1, 2)
    vsT = jnp.swapaxes(vsl, 1, 2)
    nkeys = topk * SEL_BLOCK
    tgr = jnp.transpose(rel_bias.reshape(N_BUCKETS, G, R), (1, 2, 0))
    g_ix = jnp.arange(G)[None, :, None, None, None]
    r_ix = jnp.arange(R)[None, None, :, None, None]
    gather = jax.vmap(jax.vmap(lambda a, ix: a[ix]))

    def sel_chunk(c):
        q0 = c * SEL_QCHUNK
        qs = lax.dynamic_slice_in_dim(q, q0, SEL_QCHUNK, axis=1)
        idx = lax.dynamic_slice_in_dim(sel_idx, q0, SEL_QCHUNK, axis=1)
        tok = idx[..., None] * SEL_BLOCK + jnp.arange(SEL_BLOCK)
        tok = jnp.transpose(tok.reshape(bn, SEL_QCHUNK, G, nkeys), (0, 2, 1, 3))
        flat = tok.reshape(bn, G, SEL_QCHUNK * nkeys)
        kg = gather(ksT, flat).reshape(bn, G, SEL_QCHUNK, nkeys, dh)
        vg = gather(vsT, flat).reshape(bn, G, SEL_QCHUNK, nkeys, dh)
        t = q0 + jnp.arange(SEL_QCHUNK)
        delta = t[None, None, :, None] - tok
        bias = tgr[g_ix, r_ix, _t5_bucket(delta)[:, :, None]].astype(jnp.float32)
        logits = jnp.einsum('bqgrd,bgqnd->bgrqn', qs, kg).astype(jnp.float32) * scale + bias
        prob, _, _ = _masked_softmax(logits, (delta >= 0)[:, :, None])
        return jnp.einsum('bgrqn,bgqnd->bqgrd', prob, vg.astype(jnp.float32))

    o_sel = _unblock(lax.map(sel_chunk, jnp.arange(s // SEL_QCHUNK)), s)

    o_win, _, _ = _banded_attention(q, kw, vw, rel_bias, WIN, WIN - 1, 1)

    gts = jax.nn.sigmoid(bgate.reshape(bn, s, G, R, 3).astype(jnp.float32))
    o = gts[..., 0:1] * o_cmp + gts[..., 1:2] * o_sel + gts[..., 2:3] * o_win
    y = o.reshape(bn, s, NSA_WIDTH).astype(u.dtype) * jax.nn.silu(gpath)
    return y @ w_out


def setup_inputs(seed: int = 0) -> dict:
    key = jax.random.key(seed)
    ks = iter(jax.random.split(key, 40))
    n_a = len(range(0, DEPTH, N_MIXERS))
    n_b = len(range(1, DEPTH, N_MIXERS))
    n_c = len(range(2, DEPTH, N_MIXERS))

    def nrm(shape, scale):
        return scale * jax.random.normal(next(ks), shape, jnp.float32)

    a_dec = jax.random.uniform(next(ks), (n_a, LRU_WIDTH), jnp.float32, minval=0.9, maxval=0.999)
    return {
        "x": nrm((BATCH, SEQ, D_MODEL), 1.0),
        "p": nrm((DEPTH, BATCH, SEQ, PLE_DIM), 1.0),
        "rel_bias": nrm((N_BUCKETS, N_BIAS_HEADS), 0.5),
        "norm_pre": 1.0 + nrm((DEPTH, D_MODEL), 0.05),
        "norm_post": 1.0 + nrm((DEPTH, D_MODEL), 0.05),
        "ple_w_proj": nrm((DEPTH, PLE_DIM, D_MODEL), PLE_DIM ** -0.5),
        "ple_w_gate": nrm((DEPTH, D_MODEL, D_MODEL), D_MODEL ** -0.5),
        "a_w_in": nrm((n_a, D_MODEL, 2 * LRU_WIDTH), D_MODEL ** -0.5),
        "a_conv_w": nrm((n_a, CONV_W, LRU_WIDTH), CONV_W ** -0.5),
        "a_conv_b": nrm((n_a, LRU_WIDTH), 0.01),
        "a_w_r": nrm((n_a, LRU_BLOCKS, LRU_BLOCK_W, LRU_BLOCK_W), LRU_BLOCK_W ** -0.5),
        "a_b_r": nrm((n_a, LRU_BLOCKS, LRU_BLOCK_W), 0.01),
        "a_w_i": nrm((n_a, LRU_BLOCKS, LRU_BLOCK_W, LRU_BLOCK_W), LRU_BLOCK_W ** -0.5),
        "a_b_i": nrm((n_a, LRU_BLOCKS, LRU_BLOCK_W), 0.01),
        "a_lam": jnp.log(a_dec) - jnp.log1p(-a_dec),
        "a_w_out": nrm((n_a, LRU_WIDTH, D_MODEL), LRU_WIDTH ** -0.5),
        "b_w_in": nrm((n_b, D_MODEL, DIL_IN_COLS), D_MODEL ** -0.5),
        "b_w_out": nrm((n_b, DIL_WIDTH, D_MODEL), DIL_WIDTH ** -0.5),
        "c_w_in": nrm((n_c, D_MODEL, NSA_IN_COLS), D_MODEL ** -0.5),
        "c_cmp_pos_k": nrm((n_c, CMP_BLOCK, NSA_HEAD_DIM), 0.1),
        "c_cmp_w1_k": nrm((n_c, CMP_BLOCK * NSA_HEAD_DIM, CMP_HIDDEN), (CMP_BLOCK * NSA_HEAD_DIM) ** -0.5),
        "c_cmp_w2_k": nrm((n_c, CMP_HIDDEN, NSA_HEAD_DIM), CMP_HIDDEN ** -0.5),
        "c_cmp_pos_v": nrm((n_c, CMP_BLOCK, NSA_HEAD_DIM), 0.1),
        "c_cmp_w1_v": nrm((n_c, CMP_BLOCK * NSA_HEAD_DIM, CMP_HIDDEN), (CMP_BLOCK * NSA_HEAD_DIM) ** -0.5),
        "c_cmp_w2_v": nrm((n_c, CMP_HIDDEN, NSA_HEAD_DIM), CMP_HIDDEN ** -0.5),
        "c_w_out": nrm((n_c, NSA_WIDTH, D_MODEL), NSA_WIDTH ** -0.5),
    }


def reference(x, p, rel_bias, norm_pre, norm_post, ple_w_proj, ple_w_gate,
              a_w_in, a_conv_w, a_conv_b, a_w_r, a_b_r, a_w_i, a_b_i, a_lam, a_w_out,
              b_w_in, b_w_out,
              c_w_in, c_cmp_pos_k, c_cmp_w1_k, c_cmp_w2_k, c_cmp_pos_v, c_cmp_w1_v, c_cmp_w2_v, c_w_out):
    h = x
    for i in range(DEPTH):
        kind = i % N_MIXERS
        j = i // N_MIXERS
        u = _rmsnorm(h, norm_pre[i])
        if kind == 0:
            y = _rglru_mixer(u, a_w_in[j], a_conv_w[j], a_conv_b[j], a_w_r[j], a_b_r[j],
                             a_w_i[j], a_b_i[j], a_lam[j], a_w_out[j])
        elif kind == 1:
            y = _dilated_mixer(u, b_w_in[j], b_w_out[j], rel_bias)
        else:
            y = _nsa_mixer(u, c_w_in[j], c_cmp_pos_k[j], c_cmp_w1_k[j], c_cmp_w2_k[j],
                           c_cmp_pos_v[j], c_cmp_w1_v[j], c_cmp_w2_v[j], c_w_out[j], rel_bias)
        h = h + _rmsnorm(y, norm_post[i])
        h = h + jax.nn.sigmoid(h @ ple_w_gate[i]) * (p[i] @ ple_w_proj[i])
    return h
```

```python
import functools
import math

import numpy as np
import jax
import jax.numpy as jnp
from jax import lax
from jax.experimental import pallas as pl
from jax.experimental.pallas import tpu as pltpu

F32 = jnp.float32
BF16 = jnp.bfloat16

NORM_EPS = 1e-6
NEG_INF = -1e30
TINY = 1e-30

N_MIXERS = 3
N_BUCKETS = 32
BUCKET_MAX_DIST = 2048

LRU_BLOCKS = 8
CONV_W = 4
LRU_C = 8.0

DIL_PATTERNS = ((128, 1), (512, 4), (2048, 16))
DIL_HEADS = 16
DIL_HEAD_DIM = 64
DIL_WIDTH = DIL_HEADS * DIL_HEAD_DIM

NSA_HEADS = 16
NSA_KV_GROUPS = 4
NSA_REP = NSA_HEADS // NSA_KV_GROUPS
NSA_HEAD_DIM = 128
NSA_WIDTH = NSA_HEADS * NSA_HEAD_DIM
NSA_KV_WIDTH = NSA_KV_GROUPS * NSA_HEAD_DIM
CMP_BLOCK = 32
CMP_STRIDE = 16
SEL_BLOCK = 64
SEL_TOPK = 16
WIN = 512
FORCE_BONUS = 1e4

SUBLANES = 8
LANES = 128
Q_BLOCK = 128
KEY_TILE = 256
VMEM_LIMIT_BYTES = 56 * 1024 * 1024


def _params(*sem):
    return pltpu.CompilerParams(dimension_semantics=sem, vmem_limit_bytes=VMEM_LIMIT_BYTES)


def _dot(a, b):
    return jnp.dot(a, b, preferred_element_type=F32)


def _dot_nt(a, b):
    return lax.dot_general(a, b, (((1,), (1,)), ((), ())), preferred_element_type=F32)


def _sigmoid(x):
    return 1.0 / (1.0 + jnp.exp(-x))


def _silu(x):
    return x * _sigmoid(x)


def _resident(block_shape, index_map):
    return pl.BlockSpec(block_shape, index_map, pipeline_mode=pl.Buffered(1))


def _norm_matmul_kernel(h_ref, g_ref, w_ref, o_ref, u_scr):
    @pl.when(pl.program_id(1) == 0)
    def _():
        x = h_ref[...]
        ms = jnp.mean(x * x, axis=-1, keepdims=True)
        u_scr[...] = (x * lax.rsqrt(ms + NORM_EPS) * g_ref[...]).astype(BF16)

    o_ref[...] = _dot(u_scr[...], w_ref[...]).astype(o_ref.dtype)


def _norm_matmul(h, g, w, tn):
    t, d = h.shape
    n = w.shape[1]
    tm = min(t, 1024)
    return pl.pallas_call(
        _norm_matmul_kernel,
        grid=(t // tm, n // tn),
        in_specs=[
            pl.BlockSpec((tm, d), lambda i, j: (i, 0)),
            pl.BlockSpec((1, d), lambda i, j: (0, 0)),
            pl.BlockSpec((d, tn), lambda i, j: (0, j)),
        ],
        out_specs=pl.BlockSpec((tm, tn), lambda i, j: (i, j)),
        out_shape=jax.ShapeDtypeStruct((t, n), BF16),
        scratch_shapes=[pltpu.VMEM((tm, d), BF16)],
        compiler_params=_params("parallel", "arbitrary"),
        name="norm_matmul",
    )(h, g.reshape(1, d), w)


def _out_ple_kernel(y_ref, wo_ref, g_ref, h_ref, p_ref, wp_ref, wg_ref, o_ref):
    z = _dot(y_ref[...], wo_ref[...])
    ms = jnp.mean(z * z, axis=-1, keepdims=True)
    h1 = h_ref[...] + z * lax.rsqrt(ms + NORM_EPS) * g_ref[...]
    gate = _sigmoid(_dot(h1.astype(BF16), wg_ref[...]))
    pe = _dot(p_ref[...].astype(BF16), wp_ref[...])
    o_ref[...] = h1 + gate * pe


def _out_ple(y, w_out, g_post, h, p, w_proj, w_gate):
    t, d = h.shape
    k = y.shape[1]
    pd = p.shape[1]
    tm = min(t, 256)
    return pl.pallas_call(
        _out_ple_kernel,
        grid=(t // tm,),
        in_specs=[
            pl.BlockSpec((tm, k), lambda i: (i, 0)),
            _resident((k, d), lambda i: (0, 0)),
            _resident((1, d), lambda i: (0, 0)),
            pl.BlockSpec((tm, d), lambda i: (i, 0)),
            pl.BlockSpec((tm, pd), lambda i: (i, 0)),
            _resident((pd, d), lambda i: (0, 0)),
            _resident((d, d), lambda i: (0, 0)),
        ],
        out_specs=pl.BlockSpec((tm, d), lambda i: (i, 0)),
        out_shape=jax.ShapeDtypeStruct((t, d), F32),
        compiler_params=_params("parallel"),
        name="out_ple",
    )(y, w_out, g_post.reshape(1, d), h, p, w_proj, w_gate)


def _lru_kernel(xb_ref, gate_ref, cw_ref, cb_ref, wr_ref, br_ref, wi_ref, bi_ref, lam_ref,
                y_ref, xext_scr, h_scr, *, ts, bw):
    @pl.when(pl.program_id(1) == 0)
    def _():
        xext_scr[0:SUBLANES, :] = jnp.zeros((SUBLANES, xext_scr.shape[1]), F32)
        h_scr[...] = jnp.zeros(h_scr.shape, F32)

    row = lax.broadcasted_iota(jnp.int32, (SUBLANES, bw), 0)
    for n in range(LRU_BLOCKS):
        cols = slice(n * bw, (n + 1) * bw)
        x = xb_ref[:, cols].astype(F32)
        xext_scr[SUBLANES:SUBLANES + ts, cols] = x
        xc = cw_ref[3:4, cols] * x + cb_ref[:, cols]
        for k in range(1, CONV_W):
            xc = xc + cw_ref[3 - k:4 - k, cols] * xext_scr[SUBLANES - k:SUBLANES - k + ts, cols]
        xext_scr[0:SUBLANES, cols] = x[ts - SUBLANES:ts, :]

        xn = xc.astype(BF16)
        r = _sigmoid(_dot(xn, wr_ref[n]) + br_ref[:, cols])
        ig = _sigmoid(_dot(xn, wi_ref[n]) + bi_ref[:, cols])
        nl = -lam_ref[:, cols]
        softplus = jnp.maximum(nl, 0.0) + jnp.log(1.0 + jnp.exp(-jnp.abs(nl)))
        a = jnp.exp(-LRU_C * r * softplus)
        b = jnp.sqrt(1.0 - a * a) * (ig * xc)

        h = h_scr[:, cols]
        outs = []
        for grp in range(ts // SUBLANES):
            a8 = a[grp * SUBLANES:(grp + 1) * SUBLANES, :]
            b8 = b[grp * SUBLANES:(grp + 1) * SUBLANES, :]
            for dist in (1, 2, 4):
                a_sh = jnp.where(row >= dist, pltpu.roll(a8, dist, axis=0), 1.0)
                b_sh = jnp.where(row >= dist, pltpu.roll(b8, dist, axis=0), 0.0)
                b8 = a8 * b_sh + b8
                a8 = a8 * a_sh
            hh = a8 * h + b8
            h = jnp.broadcast_to(hh[SUBLANES - 1:SUBLANES, :], (SUBLANES, bw))
            outs.append(hh)
        h_scr[:, cols] = h
        hs = jnp.concatenate(outs, axis=0)
        y_ref[:, cols] = (hs * _silu(gate_ref[:, cols].astype(F32))).astype(BF16)


def _rglru(proj, bn, s, conv_w, conv_b, w_r, b_r, w_i, b_i, lam):
    t, two_c = proj.shape
    c = two_c // 2
    bw = c // LRU_BLOCKS
    ts = min(s, 256)
    nst = s // ts
    row = lambda v: v.reshape(1, c)
    kern = functools.partial(_lru_kernel, ts=ts, bw=bw)
    return pl.pallas_call(
        kern,
        grid=(bn, nst),
        in_specs=[
            pl.BlockSpec((ts, c), lambda b, i: (b * nst + i, 0)),
            pl.BlockSpec((ts, c), lambda b, i: (b * nst + i, 1)),
            pl.BlockSpec((CONV_W, c), lambda b, i: (0, 0)),
            pl.BlockSpec((1, c), lambda b, i: (0, 0)),
            pl.BlockSpec((LRU_BLOCKS, bw, bw), lambda b, i: (0, 0, 0)),
            pl.BlockSpec((1, c), lambda b, i: (0, 0)),
            pl.BlockSpec((LRU_BLOCKS, bw, bw), lambda b, i: (0, 0, 0)),
            pl.BlockSpec((1, c), lambda b, i: (0, 0)),
            pl.BlockSpec((1, c), lambda b, i: (0, 0)),
        ],
        out_specs=pl.BlockSpec((ts, c), lambda b, i: (b * nst + i, 0)),
        out_shape=jax.ShapeDtypeStruct((t, c), BF16),
        scratch_shapes=[pltpu.VMEM((ts + SUBLANES, c), F32), pltpu.VMEM((SUBLANES, c), F32)],
        compiler_params=_params("parallel", "arbitrary"),
        name="rglru",
    )(proj, proj, conv_w, row(conv_b), w_r.astype(BF16), row(b_r), w_i.astype(BF16), row(b_i),
      row(lam))


def _t5_bucket_np(dist):
    n = np.maximum(dist, 0)
    max_exact = N_BUCKETS // 2
    nf = np.maximum(n, max_exact).astype(np.float64)
    large = max_exact + (np.log(nf / max_exact) / math.log(BUCKET_MAX_DIST / max_exact)
                         * (N_BUCKETS - max_exact)).astype(np.int64)
    return np.where(n < max_exact, n, np.minimum(large, N_BUCKETS - 1)).astype(np.int32)


def _dilated_bias(rel_bias, max_delta, dilation):
    delta = (np.arange(Q_BLOCK)[:, None] + Q_BLOCK) - np.arange(2 * Q_BLOCK)[None, :]
    ok = (delta >= 0) & (delta <= max_delta)
    tbl = jnp.transpose(rel_bias[_t5_bucket_np(delta * dilation)], (2, 0, 1))
    return jnp.where(jnp.asarray(ok)[None], tbl, NEG_INF).astype(F32)


def _causal_bias(rel_bias, n_dist):
    d = np.arange(n_dist)[:, None, None] * Q_BLOCK
    dist = d + np.arange(Q_BLOCK)[None, :, None] - np.arange(KEY_TILE)[None, None, :]
    return jnp.transpose(rel_bias[_t5_bucket_np(dist)], (3, 0, 1, 2)).astype(F32)


def _dil_kernel(q_ref, k_ref, kh_ref, v_ref, vh_ref, bias_ref, o_ref, lse_ref, kext, vext, *, nq):
    first = pl.program_id(2) == 0
    kext[0:Q_BLOCK, :] = kh_ref[...]
    kext[Q_BLOCK:, :] = k_ref[...]
    vext[0:Q_BLOCK, :] = vh_ref[...]
    vext[Q_BLOCK:, :] = v_ref[...]
    scale = DIL_HEAD_DIM ** -0.5
    lane = lax.broadcasted_iota(jnp.int32, (Q_BLOCK, LANES), 1)
    low = lane < DIL_HEAD_DIM
    col = lax.broadcasted_iota(jnp.int32, (Q_BLOCK, 2 * Q_BLOCK), 1)
    halo_pen = jnp.where(col < Q_BLOCK, jnp.where(first, NEG_INF, 0.0), 0.0)
    for j in range(nq):
        rows = slice(j * Q_BLOCK, (j + 1) * Q_BLOCK)
        band = slice(j * Q_BLOCK, (j + 2) * Q_BLOCK)
        for pair in range(DIL_HEADS // 2):
            cols = slice(pair * LANES, (pair + 1) * LANES)
            qp = q_ref[rows, cols]
            kb = kext[band, cols]
            vb = vext[band, cols]
            o_half, lse_half = [], []
            for half in range(2):
                head = 2 * pair + half
                qm = jnp.where(low if half == 0 else jnp.logical_not(low), qp, jnp.zeros_like(qp))
                sc = _dot_nt(qm, kb) * scale + bias_ref[head]
                if j == 0:
                    sc = sc + halo_pen
                m = jnp.max(sc, axis=-1, keepdims=True)
                e = jnp.exp(sc - m)
                l = jnp.sum(e, axis=-1, keepdims=True)
                pv = _dot(e.astype(BF16), vb)
                o_half.append(pv / l)
                lse_half.append(jnp.broadcast_to(m + jnp.log(l), (Q_BLOCK, LANES)))
            o_ref[rows, cols] = jnp.where(low, o_half[0], o_half[1])
            lse_ref[rows, cols] = jnp.where(low, lse_half[0], lse_half[1])


def _dil_group(proj, bn, s, gi, dil, bias):
    t, ncols = proj.shape
    w = DIL_WIDTH
    ncb = ncols // w
    length = s // dil
    tl = min(length, 2 * Q_BLOCK)
    nq = tl // Q_BLOCK
    view = proj.reshape(bn, length, dil * ncols)
    qcol, kcol, vcol = gi * 3, gi * 3 + 1, gi * 3 + 2
    halo = lambda cb: pl.BlockSpec(
        (None, Q_BLOCK, w), lambda b, r, i: (b, jnp.maximum(i * nq - 1, 0), r * ncb + cb))
    cur = lambda cb: pl.BlockSpec((None, tl, w), lambda b, r, i: (b, i, r * ncb + cb))
    out_spec = pl.BlockSpec((None, tl, w), lambda b, r, i: (b, i, r))
    o, lse = pl.pallas_call(
        functools.partial(_dil_kernel, nq=nq),
        grid=(bn, dil, length // tl),
        in_specs=[cur(qcol), cur(kcol), halo(kcol), cur(vcol), halo(vcol),
                  _resident(bias.shape, lambda b, r, i: (0, 0, 0))],
        out_specs=[out_spec, out_spec],
        out_shape=[jax.ShapeDtypeStruct((bn, length, dil * w), F32)] * 2,
        scratch_shapes=[pltpu.VMEM((tl + Q_BLOCK, w), BF16)] * 2,
        compiler_params=_params("parallel", "parallel", "arbitrary"),
        name=f"dilated_attn_{dil}",
    )(view, view, view, view, view, bias)
    return o.reshape(t, w), lse.reshape(t, w)


def _dil_merge_kernel(o0, l0, o1, l1, o2, l2, gate_ref, y_ref):
    m = jnp.maximum(jnp.maximum(l0[...], l1[...]), l2[...])
    w0 = jnp.exp(l0[...] - m)
    w1 = jnp.exp(l1[...] - m)
    w2 = jnp.exp(l2[...] - m)
    o = (w0 * o0[...] + w1 * o1[...] + w2 * o2[...]) / (w0 + w1 + w2)
    y_ref[...] = (o * _silu(gate_ref[...].astype(F32))).astype(BF16)


def _dil_merge(parts, proj):
    t, w = parts[0][0].shape
    tm = min(t, 512)
    spec = pl.BlockSpec((tm, w), lambda i: (i, 0))
    gate_cb = len(DIL_PATTERNS) * 3
    flat = [a for pair in parts for a in pair]
    return pl.pallas_call(
        _dil_merge_kernel,
        grid=(t // tm,),
        in_specs=[spec] * 6 + [pl.BlockSpec((tm, w), lambda i: (i, gate_cb))],
        out_specs=spec,
        out_shape=jax.ShapeDtypeStruct((t, w), BF16),
        compiler_params=_params("parallel"),
        name="dilated_merge",
    )(*flat, proj)


def _gelu_tanh(x):
    return 0.5 * x * (1.0 + jnp.tanh(math.sqrt(2.0 / math.pi) * (x + 0.044715 * (x * x * x))))


def _compress_kernel(x_ref, pos_ref, w1_ref, w2_ref, o_ref, *, nch):
    x = x_ref[...].astype(F32)
    top = _dot((x + pos_ref[0:1, :]).astype(BF16), w1_ref[0])
    bot = _dot((x + pos_ref[1:2, :]).astype(BF16), w1_ref[1])
    hid = top + pltpu.roll(bot, nch - 1, axis=0)
    o_ref[...] = _dot(_gelu_tanh(hid).astype(BF16), w2_ref[...]).astype(BF16)


def _compress(xkv, pos, w1, w2, nch):
    _, rows, width = xkv.shape
    hidden = w1.shape[-1]
    dh = w2.shape[-1]
    return pl.pallas_call(
        functools.partial(_compress_kernel, nch=nch),
        grid=(2, rows // nch),
        in_specs=[
            pl.BlockSpec((None, nch, width), lambda kv, i: (kv, i, 0)),
            pl.BlockSpec((None, 2, width), lambda kv, i: (kv, 0, 0)),
            pl.BlockSpec((None, 2, width, hidden), lambda kv, i: (kv, 0, 0, 0)),
            pl.BlockSpec((None, hidden, dh), lambda kv, i: (kv, 0, 0)),
        ],
        out_specs=pl.BlockSpec((None, nch, dh), lambda kv, i: (kv, i, 0)),
        out_shape=jax.ShapeDtypeStruct((2, rows, dh), BF16),
        compiler_params=_params("parallel", "parallel"),
        name="nsa_compress",
    )(xkv, pos, w1, w2)


def _flash_init(m_scr, l_scr, acc_scr):
    m_scr[...] = jnp.full(m_scr.shape, NEG_INF, F32)
    l_scr[...] = jnp.zeros(l_scr.shape, F32)
    acc_scr[...] = jnp.zeros(acc_scr.shape, F32)


def _flash_tile(qs, k, v, bias_ref, dd, keep, m_scr, l_scr, acc_scr):
    for r in range(NSA_REP):
        sc = _dot_nt(qs[r], k) + bias_ref[r, dd]
        sc = jnp.where(keep, sc, NEG_INF)
        m_prev = m_scr[r][:, 0:1]
        m_new = jnp.maximum(m_prev, jnp.max(sc, axis=-1, keepdims=True))
        alpha = jnp.exp(m_prev - m_new)
        e = jnp.where(keep, jnp.exp(sc - m_new), 0.0)
        l_new = alpha * l_scr[r][:, 0:1] + jnp.sum(e, axis=-1, keepdims=True)
        acc_scr[r] = alpha * acc_scr[r] + _dot(e.astype(BF16), v)
        m_scr[r] = jnp.broadcast_to(m_new, (Q_BLOCK, LANES))
        l_scr[r] = jnp.broadcast_to(l_new, (Q_BLOCK, LANES))


def _flash_result(r, l_scr, acc_scr):
    return acc_scr[r] / jnp.maximum(l_scr[r][:, 0:1], TINY)


def _nsa_kernel(q_ref, kc_ref, vc_ref, ks_ref, vs_ref, kw_ref, vw_ref, gp_ref, bg_ref,
                bias_ref, cov_ref, exp_ref, y_ref, m_scr, l_scr, acc_scr, *, nsel, topk, nd):
    i = pl.program_id(2)
    t0 = i * Q_BLOCK
    dh = NSA_HEAD_DIM
    scale = dh ** -0.5
    qs = [(q_ref[:, r * dh:(r + 1) * dh].astype(F32) * scale).astype(BF16) for r in range(NSA_REP)]

    kc = kc_ref[...]
    vc = vc_ref[...]
    nch = kc.shape[0]
    t_row = t0 + lax.broadcasted_iota(jnp.int32, (Q_BLOCK, nch), 0)
    cend = lax.broadcasted_iota(jnp.int32, (Q_BLOCK, nch), 1) * CMP_STRIDE + (CMP_BLOCK - 1)
    cmask = cend <= t_row
    o_cmp = []
    psum = jnp.zeros((Q_BLOCK, nch), F32)
    for r in range(NSA_REP):
        sc = jnp.where(cmask, _dot_nt(qs[r], kc), NEG_INF)
        m = jnp.max(sc, axis=-1, keepdims=True)
        e = jnp.where(cmask, jnp.exp(sc - m), 0.0)
        prob = e / jnp.maximum(jnp.sum(e, axis=-1, keepdims=True), TINY)
        o_cmp.append(_dot(prob.astype(BF16), vc))
        psum = psum + prob
    p_hi = psum.astype(BF16)
    p_lo = (psum - p_hi.astype(F32)).astype(BF16)
    cov = cov_ref[...]
    imp = (_dot_nt(cov, p_hi) + _dot_nt(cov, p_lo))[0:nsel, :]
    blk = lax.broadcasted_iota(jnp.int32, (nsel, Q_BLOCK), 0)
    t_col = t0 + lax.broadcasted_iota(jnp.int32, (nsel, Q_BLOCK), 1)
    cur = t_col // SEL_BLOCK
    forced = (blk == 0) | (blk == cur) | (blk == cur - 1)
    valid = blk * SEL_BLOCK <= t_col
    imp = jnp.where(valid, imp + jnp.where(forced, FORCE_BONUS, 0.0), NEG_INF)
    ngrp = nsel // SUBLANES
    rank = [jnp.zeros((SUBLANES, Q_BLOCK), jnp.int32) for _ in range(ngrp)]
    parts = [imp[g * SUBLANES:(g + 1) * SUBLANES, :] for g in range(ngrp)]
    sub = lax.broadcasted_iota(jnp.int32, (SUBLANES, Q_BLOCK), 0)
    for mrow in range(nsel):
        other = jnp.broadcast_to(imp[mrow:mrow + 1, :], (SUBLANES, Q_BLOCK))
        for g in range(ngrp):
            ge = jnp.where(other >= parts[g], 1, 0)
            gt = jnp.where(other > parts[g], 1, 0)
            if g * SUBLANES > mrow:
                before = ge
            elif (g + 1) * SUBLANES - 1 < mrow:
                before = gt
            else:
                before = jnp.where(sub + g * SUBLANES > mrow, ge, gt)
            rank[g] = rank[g] + before
    rank = jnp.concatenate(rank, axis=0)
    sel_t = jnp.where(valid & (rank < topk), 1.0, 0.0)
    if nsel < LANES:
        sel_t = jnp.concatenate([sel_t, jnp.zeros((LANES - nsel, Q_BLOCK), F32)], axis=0)
    sel = jnp.transpose(sel_t).astype(BF16)

    row = lax.broadcasted_iota(jnp.int32, (Q_BLOCK, KEY_TILE), 0)
    col = lax.broadcasted_iota(jnp.int32, (Q_BLOCK, KEY_TILE), 1)
    rel = row - col

    _flash_init(m_scr, l_scr, acc_scr)

    def sel_body(kt, carry):
        k0 = pl.multiple_of(kt * KEY_TILE, KEY_TILE)
        chosen = _dot(sel, exp_ref[kt])
        keep = (chosen > 0.5) & (rel + (t0 - k0) >= 0)
        dd = jnp.minimum(i - 2 * kt, nd - 1)
        _flash_tile(qs, ks_ref[pl.ds(k0, KEY_TILE), :], vs_ref[pl.ds(k0, KEY_TILE), :],
                    bias_ref, dd, keep, m_scr, l_scr, acc_scr)
        return carry

    lax.fori_loop(0, i // 2 + 1, sel_body, 0)
    o_sel = [_flash_result(r, l_scr, acc_scr) for r in range(NSA_REP)]

    _flash_init(m_scr, l_scr, acc_scr)
    for back in range(WIN // KEY_TILE + 1):
        kt_raw = i // 2 - back
        kt = jnp.maximum(kt_raw, 0)
        k0 = pl.multiple_of(kt * KEY_TILE, KEY_TILE)
        dist = rel + (t0 - k0) + jnp.where(kt_raw >= 0, 0, 2 * WIN)
        keep = (dist >= 0) & (dist <= WIN - 1)
        dd = jnp.minimum(i - 2 * kt, nd - 1)
        _flash_tile(qs, kw_ref[pl.ds(k0, KEY_TILE), :], vw_ref[pl.ds(k0, KEY_TILE), :],
                    bias_ref, dd, keep, m_scr, l_scr, acc_scr)
    o_win = [_flash_result(r, l_scr, acc_scr) for r in range(NSA_REP)]

    gates = _sigmoid(bg_ref[...].astype(F32))
    for r in range(NSA_REP):
        o = (gates[:, 3 * r:3 * r + 1] * o_cmp[r] + gates[:, 3 * r + 1:3 * r + 2] * o_sel[r]
             + gates[:, 3 * r + 2:3 * r + 3] * o_win[r])
        gp = gp_ref[:, r * dh:(r + 1) * dh].astype(F32)
        y_ref[:, r * dh:(r + 1) * dh] = (o * _silu(gp)).astype(BF16)


def _nsa_layout(s):
    nch = s // CMP_STRIDE
    nc = (s - CMP_BLOCK) // CMP_STRIDE + 1
    nsel = s // SEL_BLOCK
    start = np.arange(nc) * CMP_STRIDE
    sel_start = np.arange(nsel) * SEL_BLOCK
    cover = (start[:, None] < sel_start[None, :] + SEL_BLOCK) & (
        start[:, None] + CMP_BLOCK - 1 >= sel_start[None, :])
    cov_t = np.zeros((LANES, nch), np.float32)
    cov_t[:nsel, :nc] = cover.T
    key_blk = np.arange(s) // SEL_BLOCK
    expand = (np.arange(LANES)[:, None] == key_blk[None, :]).astype(np.float32)
    expand = expand.reshape(LANES, s // KEY_TILE, KEY_TILE).transpose(1, 0, 2)
    return jnp.asarray(cov_t, BF16), jnp.asarray(expand, BF16)


def _nsa_attention(proj, cmp_kv, bias, bn, s):
    t, ncols = proj.shape
    g_n, dh, rep = NSA_KV_GROUPS, NSA_HEAD_DIM, NSA_REP
    gw = rep * dh
    nq = s // Q_BLOCK
    nch = s // CMP_STRIDE
    nsel = s // SEL_BLOCK
    nd = bias.shape[1]
    cov_t, expand = _nsa_layout(s)
    kv0 = NSA_WIDTH // dh
    kvcol = lambda idx: pl.BlockSpec((s, dh), lambda b, g, i: (b, kv0 + idx * g_n + g))
    gp0 = (NSA_WIDTH + 6 * NSA_KV_WIDTH) // gw
    bg0 = (2 * NSA_WIDTH + 6 * NSA_KV_WIDTH) // LANES
    kern = functools.partial(_nsa_kernel, nsel=nsel, topk=min(SEL_TOPK, nsel), nd=nd)
    return pl.pallas_call(
        kern,
        grid=(bn, g_n, nq),
        in_specs=[
            pl.BlockSpec((Q_BLOCK, gw), lambda b, g, i: (b * nq + i, g)),
            pl.BlockSpec((None, nch, dh), lambda b, g, i: (0, b * g_n + g, 0)),
            pl.BlockSpec((None, nch, dh), lambda b, g, i: (1, b * g_n + g, 0)),
            kvcol(2), kvcol(3), kvcol(4), kvcol(5),
            pl.BlockSpec((Q_BLOCK, gw), lambda b, g, i: (b * nq + i, gp0 + g)),
            pl.BlockSpec((Q_BLOCK, LANES), lambda b, g, i: (b * nq + i, bg0 + g)),
            pl.BlockSpec((rep, nd, Q_BLOCK, KEY_TILE), lambda b, g, i: (g, 0, 0, 0),
                         pipeline_mode=pl.Buffered(1)),
            _resident(cov_t.shape, lambda b, g, i: (0, 0)),
            _resident(expand.shape, lambda b, g, i: (0, 0, 0)),
        ],
        out_specs=pl.BlockSpec((Q_BLOCK, gw), lambda b, g, i: (b * nq + i, g)),
        out_shape=jax.ShapeDtypeStruct((t, NSA_WIDTH), BF16),
        scratch_shapes=[pltpu.VMEM((rep, Q_BLOCK, LANES), F32),
                        pltpu.VMEM((rep, Q_BLOCK, LANES), F32),
                        pltpu.VMEM((rep, Q_BLOCK, dh), F32)],
        compiler_params=_params("parallel", "parallel", "arbitrary"),
        name="nsa_attention",
    )(proj, cmp_kv, cmp_kv, proj, proj, proj, proj, proj, proj, bias, cov_t, expand)


def _nsa_in_weight(w_in):
    main = NSA_WIDTH + 6 * NSA_KV_WIDTH
    ngate = 3 * NSA_HEADS
    per_group = ngate // NSA_KV_GROUPS
    d = w_in.shape[0]
    bg = w_in[:, main:main + ngate].reshape(d, NSA_KV_GROUPS, per_group)
    bg = jnp.pad(bg, ((0, 0), (0, 0), (0, LANES - per_group))).reshape(d, NSA_KV_GROUPS * LANES)
    return jnp.concatenate([w_in[:, :main], w_in[:, main + ngate:], bg], axis=1).astype(BF16)


def _nsa_mixer(h, g_pre, bn, s, w_in, pos_k, w1_k, w2_k, pos_v, w1_v, w2_v, rel_bias):
    t = h.shape[0]
    g_n, dh = NSA_KV_GROUPS, NSA_HEAD_DIM
    proj = _norm_matmul(h, g_pre, _nsa_in_weight(w_in), 512)
    nch = s // CMP_STRIDE
    p3 = proj.reshape(bn, nch, CMP_STRIDE, proj.shape[1])

    def chunks(c0):
        x = p3[..., c0:c0 + NSA_KV_WIDTH].reshape(bn, nch, CMP_STRIDE, g_n, dh)
        return jnp.transpose(x, (0, 3, 1, 2, 4)).reshape(bn * g_n * nch, CMP_STRIDE * dh)

    xkv = jnp.stack([chunks(NSA_WIDTH), chunks(NSA_WIDTH + NSA_KV_WIDTH)])
    half = CMP_STRIDE * dh
    pos = jnp.stack([pos_k.reshape(2, half), pos_v.reshape(2, half)])
    w1 = jnp.stack([w1_k.reshape(2, half, -1), w1_v.reshape(2, half, -1)]).astype(BF16)
    w2 = jnp.stack([w2_k, w2_v]).astype(BF16)
    cmp_kv = _compress(xkv, pos, w1, w2, nch)
    nd = min(s // Q_BLOCK, 15)
    bias = _causal_bias(rel_bias, nd)
    return _nsa_attention(proj, cmp_kv, bias, bn, s)


def kernel(x, p, rel_bias, norm_pre, norm_post, ple_w_proj, ple_w_gate, a_w_in, a_conv_w, a_conv_b,
           a_w_r, a_b_r, a_w_i, a_b_i, a_lam, a_w_out, b_w_in, b_w_out, c_w_in, c_cmp_pos_k,
           c_cmp_w1_k, c_cmp_w2_k, c_cmp_pos_v, c_cmp_w1_v, c_cmp_w2_v, c_w_out):
    bn, s, d = x.shape
    depth = p.shape[0]
    t = bn * s
    h = x.reshape(t, d)
    for i in range(depth):
        kind = i % N_MIXERS
        j = i // N_MIXERS
        if kind == 0:
            proj = _norm_matmul(h, norm_pre[i], a_w_in[j].astype(BF16), 512)
            y = _rglru(proj, bn, s, a_conv_w[j], a_conv_b[j], a_w_r[j], a_b_r[j], a_w_i[j],
                       a_b_i[j], a_lam[j])
            w_out = a_w_out[j]
        elif kind == 1:
            proj = _norm_matmul(h, norm_pre[i], b_w_in[j].astype(BF16), 512)
            parts = []
            for gi, (window, dil) in enumerate(DIL_PATTERNS):
                bias = _dilated_bias(rel_bias, window // dil, dil)
                parts.append(_dil_group(proj, bn, s, gi, dil, bias))
            y = _dil_merge(parts, proj)
            w_out = b_w_out[j]
        else:
            y = _nsa_mixer(h, norm_pre[i], bn, s, c_w_in[j], c_cmp_pos_k[j], c_cmp_w1_k[j],
                           c_cmp_w2_k[j], c_cmp_pos_v[j], c_cmp_w1_v[j], c_cmp_w2_v[j], rel_bias)
            w_out = c_w_out[j]
        h = _out_ple(y, w_out.astype(BF16), norm_post[i], h, p[i].reshape(t, -1),
                     ple_w_proj[i].astype(BF16), ple_w_gate[i].astype(BF16))
    return h.reshape(bn, s, d)
```

```python
import functools
import math

import numpy as np
import jax
import jax.numpy as jnp
from jax import lax
from jax.experimental import pallas as pl
from jax.experimental.pallas import tpu as pltpu

F32 = jnp.float32
BF16 = jnp.bfloat16

NORM_EPS = 1e-6
NEG_INF = -1e30
TINY = 1e-30

N_MIXERS = 3
N_BUCKETS = 32
BUCKET_MAX_DIST = 2048

LRU_BLOCKS = 8
CONV_W = 4
LRU_C = 8.0

DIL_PATTERNS = ((128, 1), (512, 4), (2048, 16))
DIL_HEADS = 16
DIL_HEAD_DIM = 64
DIL_WIDTH = DIL_HEADS * DIL_HEAD_DIM

NSA_HEADS = 16
NSA_KV_GROUPS = 4
NSA_REP = NSA_HEADS // NSA_KV_GROUPS
NSA_HEAD_DIM = 128
NSA_WIDTH = NSA_HEADS * NSA_HEAD_DIM
NSA_KV_WIDTH = NSA_KV_GROUPS * NSA_HEAD_DIM
CMP_BLOCK = 32
CMP_STRIDE = 16
SEL_BLOCK = 64
SEL_TOPK = 16
WIN = 512
FORCE_BONUS = 1e4

SUBLANES = 8
LANES = 128
Q_BLOCK = 128
KEY_TILE = 256
SEL_TILE = 512
WIN_SPAN = WIN + Q_BLOCK
VMEM_LIMIT_BYTES = 56 * 1024 * 1024


def _params(*sem):
    return pltpu.CompilerParams(dimension_semantics=sem, vmem_limit_bytes=VMEM_LIMIT_BYTES)


def _dot(a, b):
    return jnp.dot(a, b, preferred_element_type=F32)


def _dot_nt(a, b):
    return lax.dot_general(a, b, (((1,), (1,)), ((), ())), preferred_element_type=F32)


def _sigmoid(x):
    return 1.0 / (1.0 + jnp.exp(-x))


def _silu(x):
    return x * _sigmoid(x)


def _resident(block_shape, index_map):
    return pl.BlockSpec(block_shape, index_map, pipeline_mode=pl.Buffered(1))


def _norm_matmul_kernel(h_ref, g_ref, w_ref, o_ref, u_scr):
    @pl.when(pl.program_id(1) == 0)
    def _():
        x = h_ref[...]
        ms = jnp.mean(x * x, axis=-1, keepdims=True)
        u_scr[...] = (x * lax.rsqrt(ms + NORM_EPS) * g_ref[...]).astype(BF16)

    o_ref[...] = _dot(u_scr[...], w_ref[...]).astype(o_ref.dtype)


def _norm_matmul(h, g, w, tn):
    t, d = h.shape
    n = w.shape[1]
    tm = min(t, 1024)
    return pl.pallas_call(
        _norm_matmul_kernel,
        grid=(t // tm, n // tn),
        in_specs=[
            pl.BlockSpec((tm, d), lambda i, j: (i, 0)),
            pl.BlockSpec((1, d), lambda i, j: (0, 0)),
            pl.BlockSpec((d, tn), lambda i, j: (0, j)),
        ],
        out_specs=pl.BlockSpec((tm, tn), lambda i, j: (i, j)),
        out_shape=jax.ShapeDtypeStruct((t, n), BF16),
        scratch_shapes=[pltpu.VMEM((tm, d), BF16)],
        compiler_params=_params("parallel", "arbitrary"),
        name="norm_matmul",
    )(h, g.reshape(1, d), w)


def _out_ple_kernel(y_ref, wo_ref, g_ref, h_ref, p_ref, wp_ref, wg_ref, o_ref):
    z = _dot(y_ref[...], wo_ref[...])
    ms = jnp.mean(z * z, axis=-1, keepdims=True)
    h1 = h_ref[...] + z * lax.rsqrt(ms + NORM_EPS) * g_ref[...]
    gate = _sigmoid(_dot(h1.astype(BF16), wg_ref[...]))
    pe = _dot(p_ref[...].astype(BF16), wp_ref[...])
    o_ref[...] = h1 + gate * pe


def _out_ple(y, w_out, g_post, h, p, w_proj, w_gate):
    t, d = h.shape
    k = y.shape[1]
    pd = p.shape[1]
    tm = min(t, 256)
    return pl.pallas_call(
        _out_ple_kernel,
        grid=(t // tm,),
        in_specs=[
            pl.BlockSpec((tm, k), lambda i: (i, 0)),
            _resident((k, d), lambda i: (0, 0)),
            _resident((1, d), lambda i: (0, 0)),
            pl.BlockSpec((tm, d), lambda i: (i, 0)),
            pl.BlockSpec((tm, pd), lambda i: (i, 0)),
            _resident((pd, d), lambda i: (0, 0)),
            _resident((d, d), lambda i: (0, 0)),
        ],
        out_specs=pl.BlockSpec((tm, d), lambda i: (i, 0)),
        out_shape=jax.ShapeDtypeStruct((t, d), F32),
        compiler_params=_params("parallel"),
        name="out_ple",
    )(y, w_out, g_post.reshape(1, d), h, p, w_proj, w_gate)


def _lru_kernel(xb_ref, gate_ref, cw_ref, cb_ref, wr_ref, br_ref, wi_ref, bi_ref, lam_ref,
                y_ref, xext_scr, h_scr, *, ts, bw):
    @pl.when(pl.program_id(1) == 0)
    def _():
        xext_scr[0:SUBLANES, :] = jnp.zeros((SUBLANES, xext_scr.shape[1]), F32)
        h_scr[...] = jnp.zeros(h_scr.shape, F32)

    row = lax.broadcasted_iota(jnp.int32, (SUBLANES, bw), 0)
    for n in range(LRU_BLOCKS):
        cols = slice(n * bw, (n + 1) * bw)
        x = xb_ref[:, cols].astype(F32)
        xext_scr[SUBLANES:SUBLANES + ts, cols] = x
        xc = cw_ref[3:4, cols] * x + cb_ref[:, cols]
        for k in range(1, CONV_W):
            xc = xc + cw_ref[3 - k:4 - k, cols] * xext_scr[SUBLANES - k:SUBLANES - k + ts, cols]
        xext_scr[0:SUBLANES, cols] = x[ts - SUBLANES:ts, :]

        xn = xc.astype(BF16)
        r = _sigmoid(_dot(xn, wr_ref[n]) + br_ref[:, cols])
        ig = _sigmoid(_dot(xn, wi_ref[n]) + bi_ref[:, cols])
        nl = -lam_ref[:, cols]
        softplus = jnp.maximum(nl, 0.0) + jnp.log(1.0 + jnp.exp(-jnp.abs(nl)))
        a = jnp.exp(-LRU_C * r * softplus)
        b = jnp.sqrt(1.0 - a * a) * (ig * xc)

        h = h_scr[:, cols]
        outs = []
        for grp in range(ts // SUBLANES):
            a8 = a[grp * SUBLANES:(grp + 1) * SUBLANES, :]
            b8 = b[grp * SUBLANES:(grp + 1) * SUBLANES, :]
            for dist in (1, 2, 4):
                a_sh = jnp.where(row >= dist, pltpu.roll(a8, dist, axis=0), 1.0)
                b_sh = jnp.where(row >= dist, pltpu.roll(b8, dist, axis=0), 0.0)
                b8 = a8 * b_sh + b8
                a8 = a8 * a_sh
            hh = a8 * h + b8
            h = jnp.broadcast_to(hh[SUBLANES - 1:SUBLANES, :], (SUBLANES, bw))
            outs.append(hh)
        h_scr[:, cols] = h
        hs = jnp.concatenate(outs, axis=0)
        y_ref[:, cols] = (hs * _silu(gate_ref[:, cols].astype(F32))).astype(BF16)


def _rglru(proj, bn, s, conv_w, conv_b, w_r, b_r, w_i, b_i, lam):
    t, two_c = proj.shape
    c = two_c // 2
    bw = c // LRU_BLOCKS
    ts = min(s, 256)
    nst = s // ts
    row = lambda v: v.reshape(1, c)
    kern = functools.partial(_lru_kernel, ts=ts, bw=bw)
    return pl.pallas_call(
        kern,
        grid=(bn, nst),
        in_specs=[
            pl.BlockSpec((ts, c), lambda b, i: (b * nst + i, 0)),
            pl.BlockSpec((ts, c), lambda b, i: (b * nst + i, 1)),
            pl.BlockSpec((CONV_W, c), lambda b, i: (0, 0)),
            pl.BlockSpec((1, c), lambda b, i: (0, 0)),
            pl.BlockSpec((LRU_BLOCKS, bw, bw), lambda b, i: (0, 0, 0)),
            pl.BlockSpec((1, c), lambda b, i: (0, 0)),
            pl.BlockSpec((LRU_BLOCKS, bw, bw), lambda b, i: (0, 0, 0)),
            pl.BlockSpec((1, c), lambda b, i: (0, 0)),
            pl.BlockSpec((1, c), lambda b, i: (0, 0)),
        ],
        out_specs=pl.BlockSpec((ts, c), lambda b, i: (b * nst + i, 0)),
        out_shape=jax.ShapeDtypeStruct((t, c), BF16),
        scratch_shapes=[pltpu.VMEM((ts + SUBLANES, c), F32), pltpu.VMEM((SUBLANES, c), F32)],
        compiler_params=_params("parallel", "arbitrary"),
        name="rglru",
    )(proj, proj, conv_w, row(conv_b), w_r.astype(BF16), row(b_r), w_i.astype(BF16), row(b_i),
      row(lam))


def _t5_bucket_np(dist):
    n = np.maximum(dist, 0)
    max_exact = N_BUCKETS // 2
    nf = np.maximum(n, max_exact).astype(np.float64)
    large = max_exact + (np.log(nf / max_exact) / math.log(BUCKET_MAX_DIST / max_exact)
                         * (N_BUCKETS - max_exact)).astype(np.int64)
    return np.where(n < max_exact, n, np.minimum(large, N_BUCKETS - 1)).astype(np.int32)


def _dilated_bias(rel_bias, max_delta, dilation):
    delta = (np.arange(Q_BLOCK)[:, None] + Q_BLOCK) - np.arange(2 * Q_BLOCK)[None, :]
    ok = (delta >= 0) & (delta <= max_delta)
    tbl = jnp.transpose(rel_bias[_t5_bucket_np(delta * dilation)], (2, 0, 1))
    return jnp.where(jnp.asarray(ok)[None], tbl, NEG_INF).astype(F32)


def _causal_bias(rel_bias, n_dist):
    d = np.arange(n_dist)[:, None, None] * Q_BLOCK
    dist = d + np.arange(Q_BLOCK)[None, None, :] - np.arange(KEY_TILE)[None, :, None]
    tbl = rel_bias[_t5_bucket_np(dist)].reshape(n_dist, KEY_TILE, Q_BLOCK, NSA_KV_GROUPS, NSA_REP)
    tbl = jnp.transpose(tbl, (3, 0, 1, 4, 2))
    return tbl.reshape(NSA_KV_GROUPS, n_dist, KEY_TILE, NSA_REP * Q_BLOCK).astype(F32)


def _dil_kernel(q_ref, k_ref, kh_ref, v_ref, vh_ref, bias_ref, o_ref, lse_ref, kext, vext, *, nq):
    first = pl.program_id(2) == 0
    kext[0:Q_BLOCK, :] = kh_ref[...]
    kext[Q_BLOCK:, :] = k_ref[...]
    vext[0:Q_BLOCK, :] = vh_ref[...]
    vext[Q_BLOCK:, :] = v_ref[...]
    scale = DIL_HEAD_DIM ** -0.5
    lane = lax.broadcasted_iota(jnp.int32, (Q_BLOCK, LANES), 1)
    low = lane < DIL_HEAD_DIM
    col = lax.broadcasted_iota(jnp.int32, (Q_BLOCK, 2 * Q_BLOCK), 1)
    halo_pen = jnp.where(col < Q_BLOCK, jnp.where(first, NEG_INF, 0.0), 0.0)
    for j in range(nq):
        rows = slice(j * Q_BLOCK, (j + 1) * Q_BLOCK)
        band = slice(j * Q_BLOCK, (j + 2) * Q_BLOCK)
        for pair in range(DIL_HEADS // 2):
            cols = slice(pair * LANES, (pair + 1) * LANES)
            qp = q_ref[rows, cols]
            kb = kext[band, cols]
            vb = vext[band, cols]
            o_half, lse_half = [], []
            for half in range(2):
                head = 2 * pair + half
                qm = jnp.where(low if half == 0 else jnp.logical_not(low), qp, jnp.zeros_like(qp))
                sc = _dot_nt(qm, kb) * scale + bias_ref[head]
                if j == 0:
                    sc = sc + halo_pen
                m = jnp.max(sc, axis=-1, keepdims=True)
                e = jnp.exp(sc - m)
                l = jnp.sum(e, axis=-1, keepdims=True)
                pv = _dot(e.astype(BF16), vb)
                o_half.append(pv / l)
                lse_half.append(jnp.broadcast_to(m + jnp.log(l), (Q_BLOCK, LANES)))
            o_ref[rows, cols] = jnp.where(low, o_half[0], o_half[1])
            lse_ref[rows, cols] = jnp.where(low, lse_half[0], lse_half[1])


def _dil_group(proj, bn, s, gi, dil, bias):
    t, ncols = proj.shape
    w = DIL_WIDTH
    ncb = ncols // w
    length = s // dil
    tl = min(length, 2 * Q_BLOCK)
    nq = tl // Q_BLOCK
    view = proj.reshape(bn, length, dil * ncols)
    qcol, kcol, vcol = gi * 3, gi * 3 + 1, gi * 3 + 2
    halo = lambda cb: pl.BlockSpec(
        (None, Q_BLOCK, w), lambda b, r, i: (b, jnp.maximum(i * nq - 1, 0), r * ncb + cb))
    cur = lambda cb: pl.BlockSpec((None, tl, w), lambda b, r, i: (b, i, r * ncb + cb))
    out_spec = pl.BlockSpec((None, tl, w), lambda b, r, i: (b, i, r))
    o, lse = pl.pallas_call(
        functools.partial(_dil_kernel, nq=nq),
        grid=(bn, dil, length // tl),
        in_specs=[cur(qcol), cur(kcol), halo(kcol), cur(vcol), halo(vcol),
                  _resident(bias.shape, lambda b, r, i: (0, 0, 0))],
        out_specs=[out_spec, out_spec],
        out_shape=[jax.ShapeDtypeStruct((bn, length, dil * w), F32)] * 2,
        scratch_shapes=[pltpu.VMEM((tl + Q_BLOCK, w), BF16)] * 2,
        compiler_params=_params("parallel", "parallel", "arbitrary"),
        name=f"dilated_attn_{dil}",
    )(view, view, view, view, view, bias)
    return o.reshape(t, w), lse.reshape(t, w)


def _dil_merge_kernel(o0, l0, o1, l1, o2, l2, gate_ref, y_ref):
    m = jnp.maximum(jnp.maximum(l0[...], l1[...]), l2[...])
    w0 = jnp.exp(l0[...] - m)
    w1 = jnp.exp(l1[...] - m)
    w2 = jnp.exp(l2[...] - m)
    o = (w0 * o0[...] + w1 * o1[...] + w2 * o2[...]) / (w0 + w1 + w2)
    y_ref[...] = (o * _silu(gate_ref[...].astype(F32))).astype(BF16)


def _dil_merge(parts, proj):
    t, w = parts[0][0].shape
    tm = min(t, 512)
    spec = pl.BlockSpec((tm, w), lambda i: (i, 0))
    gate_cb = len(DIL_PATTERNS) * 3
    flat = [a for pair in parts for a in pair]
    return pl.pallas_call(
        _dil_merge_kernel,
        grid=(t // tm,),
        in_specs=[spec] * 6 + [pl.BlockSpec((tm, w), lambda i: (i, gate_cb))],
        out_specs=spec,
        out_shape=jax.ShapeDtypeStruct((t, w), BF16),
        compiler_params=_params("parallel"),
        name="dilated_merge",
    )(*flat, proj)


def _gelu_tanh(x):
    return 0.5 * x * (1.0 + jnp.tanh(math.sqrt(2.0 / math.pi) * (x + 0.044715 * (x * x * x))))


def _compress_kernel(x_ref, pos_ref, w1_ref, w2_ref, o_ref, ot_ref, *, nch):
    x = x_ref[...].astype(F32)
    top = _dot((x + pos_ref[0:1, :]).astype(BF16), w1_ref[0])
    bot = _dot((x + pos_ref[1:2, :]).astype(BF16), w1_ref[1])
    hid = top + pltpu.roll(bot, nch - 1, axis=0)
    out = _dot(_gelu_tanh(hid).astype(BF16), w2_ref[...])
    o_ref[...] = out.astype(BF16)
    ot_ref[...] = jnp.transpose(out).astype(BF16)


def _compress(xkv, pos, w1, w2, nch):
    _, rows, width = xkv.shape
    hidden = w1.shape[-1]
    dh = w2.shape[-1]
    return pl.pallas_call(
        functools.partial(_compress_kernel, nch=nch),
        grid=(2, rows // nch),
        in_specs=[
            pl.BlockSpec((None, nch, width), lambda kv, i: (kv, i, 0)),
            pl.BlockSpec((None, 2, width), lambda kv, i: (kv, 0, 0)),
            pl.BlockSpec((None, 2, width, hidden), lambda kv, i: (kv, 0, 0, 0)),
            pl.BlockSpec((None, hidden, dh), lambda kv, i: (kv, 0, 0)),
        ],
        out_specs=[pl.BlockSpec((None, nch, dh), lambda kv, i: (kv, i, 0)),
                   pl.BlockSpec((None, None, dh, nch), lambda kv, i: (kv, i, 0, 0))],
        out_shape=[jax.ShapeDtypeStruct((2, rows, dh), BF16),
                   jax.ShapeDtypeStruct((2, rows // nch, dh, nch), BF16)],
        compiler_params=_params("parallel", "parallel"),
        name="nsa_compress",
    )(xkv, pos, w1, w2)


def _nsa_kernel(q_ref, kc_ref, vct_ref, ks_ref, vst_ref, kw_ref, vwt_ref, gp_ref, bg_ref,
                bias_ref, cov_ref, exp_ref, y_ref, acc_scr, *, nsel, topk, nd):
    i = pl.program_id(2)
    t0 = i * Q_BLOCK
    dh = NSA_HEAD_DIM
    rq = NSA_REP * Q_BLOCK
    scale = dh ** -0.5
    q_all = jnp.concatenate([q_ref[:, r * dh:(r + 1) * dh] for r in range(NSA_REP)], axis=0)
    qs = (q_all.astype(F32) * scale).astype(BF16)

    def bias_tile(d):
        return bias_ref[jnp.clip(d, 0, nd - 1)]

    def per_head(x):
        return jnp.concatenate([x] * NSA_REP, axis=1)

    kc = kc_ref[...]
    nch = kc.shape[0]
    t_q = t0 + (lax.broadcasted_iota(jnp.int32, (nch, rq), 1) & (Q_BLOCK - 1))
    cend = lax.broadcasted_iota(jnp.int32, (nch, rq), 0) * CMP_STRIDE + (CMP_BLOCK - 1)
    cmask = cend <= t_q
    sc = jnp.where(cmask, _dot_nt(kc, qs), NEG_INF)
    m = jnp.max(sc, axis=0, keepdims=True)
    e = jnp.where(cmask, jnp.exp(sc - m), 0.0)
    prob = e / jnp.maximum(jnp.sum(e, axis=0, keepdims=True), TINY)
    o_cmp = _dot(vct_ref[...], prob.astype(BF16))
    psum = prob[:, 0:Q_BLOCK]
    for r in range(1, NSA_REP):
        psum = psum + prob[:, r * Q_BLOCK:(r + 1) * Q_BLOCK]
    p_hi = psum.astype(BF16)
    p_lo = (psum - p_hi.astype(F32)).astype(BF16)
    cov = cov_ref[...]
    imp = (_dot(cov, p_hi) + _dot(cov, p_lo))[0:nsel, :]
    blk = lax.broadcasted_iota(jnp.int32, (nsel, Q_BLOCK), 0)
    t_col = t0 + lax.broadcasted_iota(jnp.int32, (nsel, Q_BLOCK), 1)
    cur = t_col // SEL_BLOCK
    forced = (blk == 0) | (blk == cur) | (blk == cur - 1)
    valid = blk * SEL_BLOCK <= t_col
    imp = jnp.where(valid, imp + jnp.where(forced, FORCE_BONUS, 0.0), NEG_INF)
    ngrp = nsel // SUBLANES
    rank = [jnp.zeros((SUBLANES, Q_BLOCK), jnp.int32) for _ in range(ngrp)]
    parts = [imp[g * SUBLANES:(g + 1) * SUBLANES, :] for g in range(ngrp)]
    sub = lax.broadcasted_iota(jnp.int32, (SUBLANES, Q_BLOCK), 0)
    for mrow in range(nsel):
        other = jnp.broadcast_to(imp[mrow:mrow + 1, :], (SUBLANES, Q_BLOCK))
        for g in range(ngrp):
            ge = jnp.where(other >= parts[g], 1, 0)
            gt = jnp.where(other > parts[g], 1, 0)
            if g * SUBLANES > mrow:
                before = ge
            elif (g + 1) * SUBLANES - 1 < mrow:
                before = gt
            else:
                before = jnp.where(sub + g * SUBLANES > mrow, ge, gt)
            rank[g] = rank[g] + before
    rank = jnp.concatenate(rank, axis=0)
    sel_t = jnp.where(valid & (rank < topk), 1.0, 0.0)
    if nsel < LANES:
        sel_t = jnp.concatenate([sel_t, jnp.zeros((LANES - nsel, Q_BLOCK), F32)], axis=0)
    sel_t = sel_t.astype(BF16)

    sub_tiles = SEL_TILE // Q_BLOCK
    rel = (lax.broadcasted_iota(jnp.int32, (SEL_TILE, Q_BLOCK), 1)
           - lax.broadcasted_iota(jnp.int32, (SEL_TILE, Q_BLOCK), 0))
    acc_scr[...] = jnp.zeros(acc_scr.shape, F32)

    def sel_body(kt, carry):
        m_prev, l_prev = carry
        k0 = pl.multiple_of(kt * SEL_TILE, SEL_TILE)
        sc = _dot_nt(ks_ref[pl.ds(k0, SEL_TILE), :], qs)
        chosen = _dot(exp_ref[kt], sel_t)
        pen = jnp.where((chosen > 0.5) & (rel + (t0 - k0) >= 0), 0.0, NEG_INF)
        d0 = i - sub_tiles * kt
        bias = jnp.concatenate(
            [bias_tile(d0 - (KEY_TILE // Q_BLOCK) * u) for u in range(SEL_TILE // KEY_TILE)], axis=0)
        sc = sc + bias + per_head(pen)
        m_new = jnp.maximum(m_prev, jnp.max(sc, axis=0, keepdims=True))
        alpha = jnp.exp(m_prev - m_new)
        e = jnp.exp(sc - m_new)
        l_new = alpha * l_prev + jnp.sum(e, axis=0, keepdims=True)
        vt = jnp.concatenate([vst_ref[kt * sub_tiles + u] for u in range(sub_tiles)], axis=1)
        acc_scr[...] = alpha * acc_scr[...] + _dot(vt, e.astype(BF16))
        return m_new, l_new

    init = (jnp.full((1, rq), NEG_INF, F32), jnp.zeros((1, rq), F32))
    _, l_sel = lax.fori_loop(0, i // sub_tiles + 1, sel_body, init)
    o_sel = acc_scr[...] / jnp.maximum(l_sel, TINY)

    j0 = jnp.maximum(i - WIN // Q_BLOCK, 0)
    start = pl.multiple_of(j0 * Q_BLOCK, Q_BLOCK)
    dbase = i - j0
    sc = _dot_nt(kw_ref[pl.ds(start, WIN_SPAN), :], qs)
    dist = (dbase * Q_BLOCK + lax.broadcasted_iota(jnp.int32, (WIN_SPAN, Q_BLOCK), 1)
            - lax.broadcasted_iota(jnp.int32, (WIN_SPAN, Q_BLOCK), 0))
    pen = jnp.where((dist >= 0) & (dist <= WIN - 1), 0.0, NEG_INF)
    n_full, rem = divmod(WIN_SPAN, KEY_TILE)
    pieces = [bias_tile(dbase - (KEY_TILE // Q_BLOCK) * u) for u in range(n_full)]
    if rem:
        pieces.append(bias_tile(dbase - (KEY_TILE // Q_BLOCK) * n_full)[0:rem])
    sc = sc + jnp.concatenate(pieces, axis=0) + per_head(pen)
    m = jnp.max(sc, axis=0, keepdims=True)
    e = jnp.exp(sc - m)
    l_win = jnp.sum(e, axis=0, keepdims=True)
    vt = jnp.concatenate([vwt_ref[j0 + u] for u in range(WIN_SPAN // Q_BLOCK)], axis=1)
    o_win = _dot(vt, e.astype(BF16)) / jnp.maximum(l_win, TINY)

    gates_t = jnp.transpose(_sigmoid(bg_ref[...].astype(F32)))

    def gate_row(branch):
        return jnp.concatenate(
            [gates_t[3 * r + branch:3 * r + branch + 1, :] for r in range(NSA_REP)], axis=1)

    o_t = gate_row(0) * o_cmp + gate_row(1) * o_sel + gate_row(2) * o_win
    for r in range(NSA_REP):
        o = jnp.transpose(o_t[:, r * Q_BLOCK:(r + 1) * Q_BLOCK])
        gp = gp_ref[:, r * dh:(r + 1) * dh].astype(F32)
        y_ref[:, r * dh:(r + 1) * dh] = (o * _silu(gp)).astype(BF16)


def _nsa_layout(s):
    nch = s // CMP_STRIDE
    nc = (s - CMP_BLOCK) // CMP_STRIDE + 1
    nsel = s // SEL_BLOCK
    start = np.arange(nc) * CMP_STRIDE
    sel_start = np.arange(nsel) * SEL_BLOCK
    cover = (start[:, None] < sel_start[None, :] + SEL_BLOCK) & (
        start[:, None] + CMP_BLOCK - 1 >= sel_start[None, :])
    cov_t = np.zeros((LANES, nch), np.float32)
    cov_t[:nsel, :nc] = cover.T
    key_blk = np.arange(s) // SEL_BLOCK
    expand = (key_blk[:, None] == np.arange(LANES)[None, :]).astype(np.float32)
    expand = expand.reshape(s // SEL_TILE, SEL_TILE, LANES)
    return jnp.asarray(cov_t, BF16), jnp.asarray(expand, BF16)


def _nsa_attention(proj, cmp_kv, cmp_t, vs_t, vw_t, bias, bn, s):
    t, ncols = proj.shape
    g_n, dh, rep = NSA_KV_GROUPS, NSA_HEAD_DIM, NSA_REP
    gw = rep * dh
    nq = s // Q_BLOCK
    nch = s // CMP_STRIDE
    nsel = s // SEL_BLOCK
    nd = bias.shape[1]
    cov_t, expand = _nsa_layout(s)
    kv0 = NSA_WIDTH // dh
    kcol = lambda idx: pl.BlockSpec((s, dh), lambda b, g, i: (b, kv0 + idx * g_n + g))
    vtile = pl.BlockSpec((None, nq, dh, Q_BLOCK), lambda b, g, i: (b * g_n + g, 0, 0, 0))
    gp0 = (NSA_WIDTH + 6 * NSA_KV_WIDTH) // gw
    bg0 = (2 * NSA_WIDTH + 6 * NSA_KV_WIDTH) // LANES
    kern = functools.partial(_nsa_kernel, nsel=nsel, topk=min(SEL_TOPK, nsel), nd=nd)
    return pl.pallas_call(
        kern,
        grid=(bn, g_n, nq),
        in_specs=[
            pl.BlockSpec((Q_BLOCK, gw), lambda b, g, i: (b * nq + i, g)),
            pl.BlockSpec((None, nch, dh), lambda b, g, i: (0, b * g_n + g, 0)),
            pl.BlockSpec((None, None, dh, nch), lambda b, g, i: (1, b * g_n + g, 0, 0)),
            kcol(2), vtile, kcol(4), vtile,
            pl.BlockSpec((Q_BLOCK, gw), lambda b, g, i: (b * nq + i, gp0 + g)),
            pl.BlockSpec((Q_BLOCK, LANES), lambda b, g, i: (b * nq + i, bg0 + g)),
            pl.BlockSpec((None, nd, KEY_TILE, rep * Q_BLOCK), lambda b, g, i: (g, 0, 0, 0),
                         pipeline_mode=pl.Buffered(1)),
            _resident(cov_t.shape, lambda b, g, i: (0, 0)),
            _resident(expand.shape, lambda b, g, i: (0, 0, 0)),
        ],
        out_specs=pl.BlockSpec((Q_BLOCK, gw), lambda b, g, i: (b * nq + i, g)),
        out_shape=jax.ShapeDtypeStruct((t, NSA_WIDTH), BF16),
        scratch_shapes=[pltpu.VMEM((dh, rep * Q_BLOCK), F32)],
        compiler_params=_params("parallel", "parallel", "arbitrary"),
        name="nsa_attention",
    )(proj, cmp_kv, cmp_t, proj, vs_t, proj, vw_t, proj, proj, bias, cov_t, expand)


def _nsa_in_weight(w_in):
    main = NSA_WIDTH + 6 * NSA_KV_WIDTH
    ngate = 3 * NSA_HEADS
    per_group = ngate // NSA_KV_GROUPS
    d = w_in.shape[0]
    bg = w_in[:, main:main + ngate].reshape(d, NSA_KV_GROUPS, per_group)
    bg = jnp.pad(bg, ((0, 0), (0, 0), (0, LANES - per_group))).reshape(d, NSA_KV_GROUPS * LANES)
    return jnp.concatenate([w_in[:, :main], w_in[:, main + ngate:], bg], axis=1).astype(BF16)


def _nsa_mixer(h, g_pre, bn, s, w_in, pos_k, w1_k, w2_k, pos_v, w1_v, w2_v, rel_bias):
    t = h.shape[0]
    g_n, dh = NSA_KV_GROUPS, NSA_HEAD_DIM
    proj = _norm_matmul(h, g_pre, _nsa_in_weight(w_in), 512)
    nch = s // CMP_STRIDE
    p3 = proj.reshape(bn, nch, CMP_STRIDE, proj.shape[1])

    def chunks(c0):
        x = p3[..., c0:c0 + NSA_KV_WIDTH].reshape(bn, nch, CMP_STRIDE, g_n, dh)
        return jnp.transpose(x, (0, 3, 1, 2, 4)).reshape(bn * g_n * nch, CMP_STRIDE * dh)

    xkv = jnp.stack([chunks(NSA_WIDTH), chunks(NSA_WIDTH + NSA_KV_WIDTH)])
    half = CMP_STRIDE * dh
    pos = jnp.stack([pos_k.reshape(2, half), pos_v.reshape(2, half)])
    w1 = jnp.stack([w1_k.reshape(2, half, -1), w1_v.reshape(2, half, -1)]).astype(BF16)
    w2 = jnp.stack([w2_k, w2_v]).astype(BF16)
    cmp_kv, cmp_t = _compress(xkv, pos, w1, w2, nch)

    def value_tiles_t(c0):
        x = proj.reshape(bn, s // Q_BLOCK, Q_BLOCK, proj.shape[1])[..., c0:c0 + NSA_KV_WIDTH]
        x = x.reshape(bn, s // Q_BLOCK, Q_BLOCK, g_n, dh)
        return jnp.transpose(x, (0, 3, 1, 4, 2)).reshape(bn * g_n, s // Q_BLOCK, dh, Q_BLOCK)

    vs_t = value_tiles_t(NSA_WIDTH + 3 * NSA_KV_WIDTH)
    vw_t = value_tiles_t(NSA_WIDTH + 5 * NSA_KV_WIDTH)
    nd = min(s // Q_BLOCK, 15)
    bias = _causal_bias(rel_bias, nd)
    return _nsa_attention(proj, cmp_kv, cmp_t, vs_t, vw_t, bias, bn, s)


def kernel(x, p, rel_bias, norm_pre, norm_post, ple_w_proj, ple_w_gate, a_w_in, a_conv_w, a_conv_b,
           a_w_r, a_b_r, a_w_i, a_b_i, a_lam, a_w_out, b_w_in, b_w_out, c_w_in, c_cmp_pos_k,
           c_cmp_w1_k, c_cmp_w2_k, c_cmp_pos_v, c_cmp_w1_v, c_cmp_w2_v, c_w_out):
    bn, s, d = x.shape
    depth = p.shape[0]
    t = bn * s
    h = x.reshape(t, d)
    for i in range(depth):
        kind = i % N_MIXERS
        j = i // N_MIXERS
        if kind == 0:
            proj = _norm_matmul(h, norm_pre[i], a_w_in[j].astype(BF16), 512)
            y = _rglru(proj, bn, s, a_conv_w[j], a_conv_b[j], a_w_r[j], a_b_r[j], a_w_i[j],
                       a_b_i[j], a_lam[j])
            w_out = a_w_out[j]
        elif kind == 1:
            proj = _norm_matmul(h, norm_pre[i], b_w_in[j].astype(BF16), 512)
            parts = []
            for gi, (window, dil) in enumerate(DIL_PATTERNS):
                bias = _dilated_bias(rel_bias, window // dil, dil)
                parts.append(_dil_group(proj, bn, s, gi, dil, bias))
            y = _dil_merge(parts, proj)
            w_out = b_w_out[j]
        else:
            y = _nsa_mixer(h, norm_pre[i], bn, s, c_w_in[j], c_cmp_pos_k[j], c_cmp_w1_k[j],
                           c_cmp_w2_k[j], c_cmp_pos_v[j], c_cmp_w1_v[j], c_cmp_w2_v[j], rel_bias)
            w_out = c_w_out[j]
        h = _out_ple(y, w_out.astype(BF16), norm_post[i], h, p[i].reshape(t, -1),
                     ple_w_proj[i].astype(BF16), ple_w_gate[i].astype(BF16))
    return h.reshape(bn, s, d)
```

```python
import functools
import math

import numpy as np
import jax
import jax.numpy as jnp
from jax import lax
from jax.experimental import pallas as pl
from jax.experimental.pallas import tpu as pltpu

F32 = jnp.float32
BF16 = jnp.bfloat16

NORM_EPS = 1e-6
NEG_INF = -1e30
TINY = 1e-30
MASK_BIG = 2.0 ** 40

N_MIXERS = 3
N_BUCKETS = 32
BUCKET_MAX_DIST = 2048

LRU_BLOCKS = 8
CONV_W = 4
LRU_C = 8.0

DIL_PATTERNS = ((128, 1), (512, 4), (2048, 16))
DIL_HEADS = 16
DIL_HEAD_DIM = 64
DIL_WIDTH = DIL_HEADS * DIL_HEAD_DIM

NSA_HEADS = 16
NSA_KV_GROUPS = 4
NSA_REP = NSA_HEADS // NSA_KV_GROUPS
NSA_HEAD_DIM = 128
NSA_WIDTH = NSA_HEADS * NSA_HEAD_DIM
NSA_KV_WIDTH = NSA_KV_GROUPS * NSA_HEAD_DIM
CMP_BLOCK = 32
CMP_STRIDE = 16
SEL_BLOCK = 64
SEL_TOPK = 16
WIN = 512
FORCE_BONUS = 1e4

SUBLANES = 8
LANES = 128
Q_BLOCK = 128
KEY_TILE = 256
DIL_WINDOW = Q_BLOCK * max(d for _, d in DIL_PATTERNS)
SEL_TILE = 512
WIN_SPAN = WIN + Q_BLOCK
VMEM_LIMIT_BYTES = 56 * 1024 * 1024


def _params(*sem):
    return pltpu.CompilerParams(dimension_semantics=sem, vmem_limit_bytes=VMEM_LIMIT_BYTES)


def _dot(a, b):
    return jnp.dot(a, b, preferred_element_type=F32)


def _dot_nt(a, b):
    return lax.dot_general(a, b, (((1,), (1,)), ((), ())), preferred_element_type=F32)


def _sigmoid(x):
    return 1.0 / (1.0 + jnp.exp(-x))


def _silu(x):
    return x * _sigmoid(x)


def _resident(block_shape, index_map):
    return pl.BlockSpec(block_shape, index_map, pipeline_mode=pl.Buffered(1))


def _norm_matmul_kernel(h_ref, g_ref, w_ref, o_ref, u_scr):
    @pl.when(pl.program_id(1) == 0)
    def _():
        x = h_ref[...]
        ms = jnp.mean(x * x, axis=-1, keepdims=True)
        u_scr[...] = (x * lax.rsqrt(ms + NORM_EPS) * g_ref[...]).astype(BF16)

    o_ref[...] = _dot(u_scr[...], w_ref[...]).astype(o_ref.dtype)


def _norm_matmul_strided_kernel(h_ref, g_ref, w_ref, o_ref, u_scr, res_scr, *, dil):
    @pl.when(pl.program_id(1) == 0)
    def _():
        x = h_ref[...]
        ms = jnp.mean(x * x, axis=-1, keepdims=True)
        u_scr[...] = (x * lax.rsqrt(ms + NORM_EPS) * g_ref[...]).astype(BF16)

    res = _dot(u_scr[...], w_ref[...])
    tm, tn = res.shape
    for c in range(tn // LANES):
        res_scr[c] = res[:, c * LANES:(c + 1) * LANES]
    for r in range(dil):
        for c in range(tn // LANES):
            o_ref[r, :, c * LANES:(c + 1) * LANES] = (
                res_scr[c, pl.ds(r, tm // dil, stride=dil), :].astype(o_ref.dtype))


def _norm_matmul(h, g, w, tn, dil=1, bn=1):
    t, d = h.shape
    n = w.shape[1]
    tm = min(t, 1024)
    if dil > 1:
        s = t // bn
        nt = s // tm
        return pl.pallas_call(
            functools.partial(_norm_matmul_strided_kernel, dil=dil),
            grid=(t // tm, n // tn),
            in_specs=[
                pl.BlockSpec((tm, d), lambda i, j: (i, 0)),
                pl.BlockSpec((1, d), lambda i, j: (0, 0)),
                pl.BlockSpec((d, tn), lambda i, j: (0, j)),
            ],
            out_specs=pl.BlockSpec((None, dil, tm // dil, tn), lambda i, j: (i // nt, 0, i % nt, j)),
            out_shape=jax.ShapeDtypeStruct((bn, dil, s // dil, n), BF16),
            scratch_shapes=[pltpu.VMEM((tm, d), BF16), pltpu.VMEM((tn // LANES, tm, LANES), F32)],
            compiler_params=_params("parallel", "arbitrary"),
            name=f"norm_matmul_residue_{dil}",
        )(h, g.reshape(1, d), w)
    return pl.pallas_call(
        _norm_matmul_kernel,
        grid=(t // tm, n // tn),
        in_specs=[
            pl.BlockSpec((tm, d), lambda i, j: (i, 0)),
            pl.BlockSpec((1, d), lambda i, j: (0, 0)),
            pl.BlockSpec((d, tn), lambda i, j: (0, j)),
        ],
        out_specs=pl.BlockSpec((tm, tn), lambda i, j: (i, j)),
        out_shape=jax.ShapeDtypeStruct((t, n), BF16),
        scratch_shapes=[pltpu.VMEM((tm, d), BF16)],
        compiler_params=_params("parallel", "arbitrary"),
        name="norm_matmul",
    )(h, g.reshape(1, d), w)


def _out_ple_kernel(y_ref, wo_ref, g_ref, h_ref, p_ref, wp_ref, wg_ref, o_ref):
    z = _dot(y_ref[...], wo_ref[...])
    ms = jnp.mean(z * z, axis=-1, keepdims=True)
    h1 = h_ref[...] + z * lax.rsqrt(ms + NORM_EPS) * g_ref[...]
    gate = _sigmoid(_dot(h1.astype(BF16), wg_ref[...]))
    pe = _dot(p_ref[...].astype(BF16), wp_ref[...])
    o_ref[...] = h1 + gate * pe


def _out_ple(y, w_out, g_post, h, p, w_proj, w_gate):
    t, d = h.shape
    k = y.shape[1]
    pd = p.shape[1]
    tm = min(t, 256)
    return pl.pallas_call(
        _out_ple_kernel,
        grid=(t // tm,),
        in_specs=[
            pl.BlockSpec((tm, k), lambda i: (i, 0)),
            _resident((k, d), lambda i: (0, 0)),
            _resident((1, d), lambda i: (0, 0)),
            pl.BlockSpec((tm, d), lambda i: (i, 0)),
            pl.BlockSpec((tm, pd), lambda i: (i, 0)),
            _resident((pd, d), lambda i: (0, 0)),
            _resident((d, d), lambda i: (0, 0)),
        ],
        out_specs=pl.BlockSpec((tm, d), lambda i: (i, 0)),
        out_shape=jax.ShapeDtypeStruct((t, d), F32),
        compiler_params=_params("parallel"),
        name="out_ple",
    )(y, w_out, g_post.reshape(1, d), h, p, w_proj, w_gate)


def _lru_kernel(xb_ref, gate_ref, cw_ref, cb_ref, wr_ref, br_ref, wi_ref, bi_ref, lam_ref,
                y_ref, xext_scr, h_scr, *, ts, bw):
    @pl.when(pl.program_id(1) == 0)
    def _():
        xext_scr[0:SUBLANES, :] = jnp.zeros((SUBLANES, xext_scr.shape[1]), F32)
        h_scr[...] = jnp.zeros(h_scr.shape, F32)

    row = lax.broadcasted_iota(jnp.int32, (SUBLANES, bw), 0)
    for n in range(LRU_BLOCKS):
        cols = slice(n * bw, (n + 1) * bw)
        x = xb_ref[:, cols].astype(F32)
        xext_scr[SUBLANES:SUBLANES + ts, cols] = x
        xc = cw_ref[3:4, cols] * x + cb_ref[:, cols]
        for k in range(1, CONV_W):
            xc = xc + cw_ref[3 - k:4 - k, cols] * xext_scr[SUBLANES - k:SUBLANES - k + ts, cols]
        xext_scr[0:SUBLANES, cols] = x[ts - SUBLANES:ts, :]

        xn = xc.astype(BF16)
        r = _sigmoid(_dot(xn, wr_ref[n]) + br_ref[:, cols])
        ig = _sigmoid(_dot(xn, wi_ref[n]) + bi_ref[:, cols])
        nl = -lam_ref[:, cols]
        softplus = jnp.maximum(nl, 0.0) + jnp.log(1.0 + jnp.exp(-jnp.abs(nl)))
        a = jnp.exp(-LRU_C * r * softplus)
        b = jnp.sqrt(1.0 - a * a) * (ig * xc)

        h = h_scr[:, cols]
        outs = []
        for grp in range(ts // SUBLANES):
            a8 = a[grp * SUBLANES:(grp + 1) * SUBLANES, :]
            b8 = b[grp * SUBLANES:(grp + 1) * SUBLANES, :]
            for dist in (1, 2, 4):
                a_sh = jnp.where(row >= dist, pltpu.roll(a8, dist, axis=0), 1.0)
                b_sh = jnp.where(row >= dist, pltpu.roll(b8, dist, axis=0), 0.0)
                b8 = a8 * b_sh + b8
                a8 = a8 * a_sh
            hh = a8 * h + b8
            h = jnp.broadcast_to(hh[SUBLANES - 1:SUBLANES, :], (SUBLANES, bw))
            outs.append(hh)
        h_scr[:, cols] = h
        hs = jnp.concatenate(outs, axis=0)
        y_ref[:, cols] = (hs * _silu(gate_ref[:, cols].astype(F32))).astype(BF16)


def _rglru(proj, bn, s, conv_w, conv_b, w_r, b_r, w_i, b_i, lam):
    t, two_c = proj.shape
    c = two_c // 2
    bw = c // LRU_BLOCKS
    ts = min(s, 256)
    nst = s // ts
    row = lambda v: v.reshape(1, c)
    kern = functools.partial(_lru_kernel, ts=ts, bw=bw)
    return pl.pallas_call(
        kern,
        grid=(bn, nst),
        in_specs=[
            pl.BlockSpec((ts, c), lambda b, i: (b * nst + i, 0)),
            pl.BlockSpec((ts, c), lambda b, i: (b * nst + i, 1)),
            pl.BlockSpec((CONV_W, c), lambda b, i: (0, 0)),
            pl.BlockSpec((1, c), lambda b, i: (0, 0)),
            pl.BlockSpec((LRU_BLOCKS, bw, bw), lambda b, i: (0, 0, 0)),
            pl.BlockSpec((1, c), lambda b, i: (0, 0)),
            pl.BlockSpec((LRU_BLOCKS, bw, bw), lambda b, i: (0, 0, 0)),
            pl.BlockSpec((1, c), lambda b, i: (0, 0)),
            pl.BlockSpec((1, c), lambda b, i: (0, 0)),
        ],
        out_specs=pl.BlockSpec((ts, c), lambda b, i: (b * nst + i, 0)),
        out_shape=jax.ShapeDtypeStruct((t, c), BF16),
        scratch_shapes=[pltpu.VMEM((ts + SUBLANES, c), F32), pltpu.VMEM((SUBLANES, c), F32)],
        compiler_params=_params("parallel", "arbitrary"),
        name="rglru",
    )(proj, proj, conv_w, row(conv_b), w_r.astype(BF16), row(b_r), w_i.astype(BF16), row(b_i),
      row(lam))


def _t5_bucket_np(dist):
    n = np.maximum(dist, 0)
    max_exact = N_BUCKETS // 2
    nf = np.maximum(n, max_exact).astype(np.float64)
    large = max_exact + (np.log(nf / max_exact) / math.log(BUCKET_MAX_DIST / max_exact)
                         * (N_BUCKETS - max_exact)).astype(np.int64)
    return np.where(n < max_exact, n, np.minimum(large, N_BUCKETS - 1)).astype(np.int32)


def _dilated_bias(rel_bias, max_delta, dilation):
    delta = (np.arange(Q_BLOCK)[:, None] + Q_BLOCK) - np.arange(2 * Q_BLOCK)[None, :]
    ok = (delta >= 0) & (delta <= max_delta)
    tbl = jnp.transpose(rel_bias[_t5_bucket_np(delta * dilation)], (2, 0, 1))
    return jnp.where(jnp.asarray(ok)[None], tbl, NEG_INF).astype(F32)


def _causal_bias(rel_bias, n_dist):
    span = Q_BLOCK + KEY_TILE
    heads = rel_bias.shape[1]
    dist = np.arange(n_dist)[:, None] * Q_BLOCK + (Q_BLOCK - 1) - np.arange(span)[None, :]
    vec = jnp.transpose(rel_bias[_t5_bucket_np(dist)], (0, 2, 1)).astype(F32)
    vec = jnp.where(jnp.asarray(dist >= 0)[:, None, :], vec, NEG_INF)
    vec = jnp.concatenate([vec, jnp.full((1, heads, span), NEG_INF, F32)])
    flat = jnp.tile(vec, (1, 1, Q_BLOCK + 1))[..., :Q_BLOCK * (span + 1)]
    hankel = flat.reshape(n_dist + 1, heads, Q_BLOCK, span + 1)[..., :KEY_TILE]
    tbl = hankel[:, :, ::-1, :].reshape(n_dist + 1, NSA_KV_GROUPS, NSA_REP, Q_BLOCK, KEY_TILE)
    tbl = jnp.transpose(tbl, (1, 0, 4, 2, 3))
    return tbl.reshape(NSA_KV_GROUPS, n_dist + 1, KEY_TILE, NSA_REP * Q_BLOCK)


def _dil_kernel(*refs, dils):
    ng = len(dils)
    groups = [refs[5 * g:5 * g + 5] for g in range(ng)]
    bias_refs = refs[5 * ng:6 * ng]
    gate_ref, y_ref = refs[6 * ng], refs[6 * ng + 1]
    ext = refs[6 * ng + 2:8 * ng + 2]
    o_acc, lse_acc = refs[8 * ng + 2:]
    first = pl.program_id(1) == 0
    scale = DIL_HEAD_DIM ** -0.5
    lane = lax.broadcasted_iota(jnp.int32, (Q_BLOCK, LANES), 1)
    low = lane < DIL_HEAD_DIM
    col = lax.broadcasted_iota(jnp.int32, (Q_BLOCK, 2 * Q_BLOCK), 1)
    halo_pen = jnp.where(col < Q_BLOCK, jnp.where(first, NEG_INF, 0.0), 0.0)
    for g, dil in enumerate(dils):
        q_ref, k_ref, kh_ref, v_ref, vh_ref = groups[g]
        kext, vext = ext[2 * g], ext[2 * g + 1]
        kext[:, 0:Q_BLOCK, :] = kh_ref[...]
        kext[:, Q_BLOCK:, :] = k_ref[...]
        vext[:, 0:Q_BLOCK, :] = vh_ref[...]
        vext[:, Q_BLOCK:, :] = v_ref[...]
        for r in range(dil):
            for j in range(k_ref.shape[1] // Q_BLOCK):
                qp = q_ref[r, j * Q_BLOCK:(j + 1) * Q_BLOCK, :]
                kb = kext[r, j * Q_BLOCK:(j + 2) * Q_BLOCK, :]
                vb = vext[r, j * Q_BLOCK:(j + 2) * Q_BLOCK, :]
                o_half, lse_half = [], []
                for half in range(2):
                    qm = jnp.where(low if half == 0 else jnp.logical_not(low), qp, jnp.zeros_like(qp))
                    sc = _dot_nt(qm, kb) * scale + bias_refs[g][half]
                    if j == 0:
                        sc = sc + halo_pen
                    m = jnp.max(sc, axis=-1, keepdims=True)
                    e = jnp.exp(sc - m)
                    l = jnp.sum(e, axis=-1, keepdims=True)
                    pv = _dot(e.astype(BF16), vb)
                    o_half.append(pv / l)
                    lse_half.append(jnp.broadcast_to(m + jnp.log(l), (Q_BLOCK, LANES)))
                o_new = jnp.where(low, o_half[0], o_half[1])
                lse_new = jnp.where(low, lse_half[0], lse_half[1])
                if dil == 1:
                    rows = pl.ds(j * Q_BLOCK, Q_BLOCK)
                else:
                    rows = pl.ds(r + dil * j * Q_BLOCK, Q_BLOCK, stride=dil)
                if g == 0:
                    o_acc[rows, :] = o_new
                    lse_acc[rows, :] = lse_new
                else:
                    o_old = o_acc[rows, :]
                    lse_old = lse_acc[rows, :]
                    top = jnp.maximum(lse_old, lse_new)
                    w_old = jnp.exp(lse_old - top)
                    w_new = jnp.exp(lse_new - top)
                    tot = w_old + w_new
                    o_acc[rows, :] = (w_old * o_old + w_new * o_new) / tot
                    if g + 1 < ng:
                        lse_acc[rows, :] = top + jnp.log(tot)
    y_ref[...] = (o_acc[...] * _silu(gate_ref[...].astype(F32))).astype(BF16)


def _dil_attention(arrays, biases, bn, s):
    dils = tuple(d for _, d in DIL_PATTERNS)
    win = DIL_WINDOW
    pairs = DIL_WIDTH // LANES
    nw = s // win
    in_specs, operands, scratch = [], [], []
    for g, dil in enumerate(dils):
        per = win // dil
        per_blocks = per // Q_BLOCK
        cur = lambda cb, dil=dil, per=per: pl.BlockSpec(
            (None, dil, per, LANES), lambda b, w, p: (b, 0, w, cb * pairs + p))
        halo = lambda cb, dil=dil, per_blocks=per_blocks: pl.BlockSpec(
            (None, dil, Q_BLOCK, LANES),
            lambda b, w, p: (b, 0, jnp.maximum(w * per_blocks - 1, 0), cb * pairs + p))
        in_specs += [cur(0), cur(1), halo(1), cur(2), halo(2)]
        operands += [arrays[g]] * 5
        scratch += [pltpu.VMEM((dil, per + Q_BLOCK, LANES), BF16)] * 2
    in_specs += [pl.BlockSpec((2, Q_BLOCK, 2 * Q_BLOCK), lambda b, w, p: (p, 0, 0))] * len(dils)
    operands += list(biases)
    in_specs.append(pl.BlockSpec((None, None, win, LANES), lambda b, w, p: (b, 0, w, 3 * pairs + p)))
    operands.append(arrays[0])
    scratch += [pltpu.VMEM((win, LANES), F32)] * 2
    return pl.pallas_call(
        functools.partial(_dil_kernel, dils=dils),
        grid=(bn, nw, pairs),
        in_specs=in_specs,
        out_specs=pl.BlockSpec((win, LANES), lambda b, w, p: (b * nw + w, p)),
        out_shape=jax.ShapeDtypeStruct((bn * s, DIL_WIDTH), BF16),
        scratch_shapes=scratch,
        compiler_params=_params("parallel", "arbitrary", "arbitrary"),
        name="dilated_attention",
    )(*operands)


def _dilated_mixer(h, g_pre, bn, s, w_in, rel_bias):
    w = DIL_WIDTH
    ng = len(DIL_PATTERNS)
    wb = w_in.astype(BF16)
    main_w = jnp.concatenate([wb[:, :3 * w], wb[:, 3 * ng * w:]], axis=1)
    arrays = [_norm_matmul(h, g_pre, main_w, 512).reshape(bn, 1, s, 4 * w)]
    for gi in range(1, ng):
        dil = DIL_PATTERNS[gi][1]
        arrays.append(_norm_matmul(h, g_pre, wb[:, 3 * gi * w:3 * (gi + 1) * w], 512, dil=dil, bn=bn))
    biases = [_dilated_bias(rel_bias, window // dil, dil) for window, dil in DIL_PATTERNS]
    return _dil_attention(arrays, biases, bn, s)


def _gelu_tanh(x):
    return 0.5 * x * (1.0 + jnp.tanh(math.sqrt(2.0 / math.pi) * (x + 0.044715 * (x * x * x))))


def _compress_kernel(x_ref, pos_ref, w1_ref, w2_ref, o_ref, ot_ref, *, nch):
    x = x_ref[...].astype(F32)
    top = _dot((x + pos_ref[0:1, :]).astype(BF16), w1_ref[0])
    bot = _dot((x + pos_ref[1:2, :]).astype(BF16), w1_ref[1])
    hid = top + pltpu.roll(bot, nch - 1, axis=0)
    out = _dot(_gelu_tanh(hid).astype(BF16), w2_ref[...])
    o_ref[...] = out.astype(BF16)
    ot_ref[...] = jnp.transpose(out).astype(BF16)


def _compress(xkv, pos, w1, w2, nch):
    _, rows, width = xkv.shape
    hidden = w1.shape[-1]
    dh = w2.shape[-1]
    return pl.pallas_call(
        functools.partial(_compress_kernel, nch=nch),
        grid=(2, rows // nch),
        in_specs=[
            pl.BlockSpec((None, nch, width), lambda kv, i: (kv, i, 0)),
            pl.BlockSpec((None, 2, width), lambda kv, i: (kv, 0, 0)),
            pl.BlockSpec((None, 2, width, hidden), lambda kv, i: (kv, 0, 0, 0)),
            pl.BlockSpec((None, hidden, dh), lambda kv, i: (kv, 0, 0)),
        ],
        out_specs=[pl.BlockSpec((None, nch, dh), lambda kv, i: (kv, i, 0)),
                   pl.BlockSpec((None, None, dh, nch), lambda kv, i: (kv, i, 0, 0))],
        out_shape=[jax.ShapeDtypeStruct((2, rows, dh), BF16),
                   jax.ShapeDtypeStruct((2, rows // nch, dh, nch), BF16)],
        compiler_params=_params("parallel", "parallel"),
        name="nsa_compress",
    )(xkv, pos, w1, w2)


def _nsa_kernel(q_ref, kc_ref, vct_ref, ks_ref, vst_ref, kw_ref, vwt_ref, gp_ref, bg_ref,
                bias_ref, cov_ref, exp_ref, y_ref, acc_scr, *, nsel, topk, nd):
    i = pl.program_id(2)
    t0 = i * Q_BLOCK
    dh = NSA_HEAD_DIM
    rq = NSA_REP * Q_BLOCK
    scale = dh ** -0.5
    q_all = jnp.concatenate([q_ref[:, r * dh:(r + 1) * dh] for r in range(NSA_REP)], axis=0)
    qs = (q_all.astype(F32) * scale).astype(BF16)

    def bias_tile(d):
        return bias_ref[jnp.where(d < 0, nd, jnp.minimum(d, nd - 1))]

    def per_head(x):
        return jnp.concatenate([x] * NSA_REP, axis=1)

    kc = kc_ref[...]
    nch = kc.shape[0]
    t_q = t0 + (lax.broadcasted_iota(jnp.int32, (nch, rq), 1) & (Q_BLOCK - 1))
    cend = lax.broadcasted_iota(jnp.int32, (nch, rq), 0) * CMP_STRIDE + (CMP_BLOCK - 1)
    cmask = cend <= t_q
    sc = jnp.where(cmask, _dot_nt(kc, qs), NEG_INF)
    m = jnp.max(sc, axis=0, keepdims=True)
    e = jnp.where(cmask, jnp.exp(sc - m), 0.0)
    prob = e / jnp.maximum(jnp.sum(e, axis=0, keepdims=True), TINY)
    o_cmp = _dot(vct_ref[...], prob.astype(BF16))
    psum = prob[:, 0:Q_BLOCK]
    for r in range(1, NSA_REP):
        psum = psum + prob[:, r * Q_BLOCK:(r + 1) * Q_BLOCK]
    p_hi = psum.astype(BF16)
    p_lo = (psum - p_hi.astype(F32)).astype(BF16)
    cov = cov_ref[...]
    imp = (_dot(cov, p_hi) + _dot(cov, p_lo))[0:nsel, :]
    blk = lax.broadcasted_iota(jnp.int32, (nsel, Q_BLOCK), 0)
    t_col = t0 + lax.broadcasted_iota(jnp.int32, (nsel, Q_BLOCK), 1)
    cur = t_col // SEL_BLOCK
    forced = (blk == 0) | (blk == cur) | (blk == cur - 1)
    valid = blk * SEL_BLOCK <= t_col
    imp = jnp.where(valid, imp + jnp.where(forced, FORCE_BONUS, 0.0), NEG_INF)
    ngrp = nsel // SUBLANES
    rank = [jnp.zeros((SUBLANES, Q_BLOCK), jnp.int32) for _ in range(ngrp)]
    parts = [imp[g * SUBLANES:(g + 1) * SUBLANES, :] for g in range(ngrp)]
    sub = lax.broadcasted_iota(jnp.int32, (SUBLANES, Q_BLOCK), 0)
    for mrow in range(nsel):
        other = jnp.broadcast_to(imp[mrow:mrow + 1, :], (SUBLANES, Q_BLOCK))
        for g in range(ngrp):
            ge = jnp.where(other >= parts[g], 1, 0)
            gt = jnp.where(other > parts[g], 1, 0)
            if g * SUBLANES > mrow:
                before = ge
            elif (g + 1) * SUBLANES - 1 < mrow:
                before = gt
            else:
                before = jnp.where(sub + g * SUBLANES > mrow, ge, gt)
            rank[g] = rank[g] + before
    rank = jnp.concatenate(rank, axis=0)
    sel_t = jnp.where(valid & (rank < topk), 1.0, 0.0)
    if nsel < LANES:
        sel_t = jnp.concatenate([sel_t, jnp.zeros((LANES - nsel, Q_BLOCK), F32)], axis=0)
    sub_tiles = SEL_TILE // Q_BLOCK
    sel_pen = jnp.transpose(jnp.where(sel_t > 0.5, 0.0, -MASK_BIG)).astype(BF16)
    q_aug = jnp.concatenate([qs, jnp.concatenate([sel_pen] * NSA_REP, axis=0)], axis=1)
    acc_scr[...] = jnp.zeros(acc_scr.shape, F32)

    def sel_body(kt, carry):
        m_prev, l_prev = carry
        k0 = pl.multiple_of(kt * SEL_TILE, SEL_TILE)
        k_aug = jnp.concatenate([ks_ref[pl.ds(k0, SEL_TILE), :], exp_ref[kt]], axis=1)
        d0 = i - sub_tiles * kt
        bias = jnp.concatenate(
            [bias_tile(d0 - (KEY_TILE // Q_BLOCK) * u) for u in range(SEL_TILE // KEY_TILE)], axis=0)
        sc = _dot_nt(k_aug, q_aug) + bias
        m_new = jnp.maximum(m_prev, jnp.max(sc, axis=0, keepdims=True))
        alpha = jnp.exp(m_prev - m_new)
        e = jnp.exp(sc - m_new)
        l_new = alpha * l_prev + jnp.sum(e, axis=0, keepdims=True)
        vt = jnp.concatenate([vst_ref[kt * sub_tiles + u] for u in range(sub_tiles)], axis=1)
        acc_scr[...] = alpha * acc_scr[...] + _dot(vt, e.astype(BF16))
        return m_new, l_new

    init = (jnp.full((1, rq), NEG_INF, F32), jnp.zeros((1, rq), F32))
    _, l_sel = lax.fori_loop(0, i // sub_tiles + 1, sel_body, init)
    o_sel = acc_scr[...] / jnp.maximum(l_sel, TINY)

    j0 = jnp.maximum(i - WIN // Q_BLOCK, 0)
    start = pl.multiple_of(j0 * Q_BLOCK, Q_BLOCK)
    dbase = i - j0
    sc = _dot_nt(kw_ref[pl.ds(start, WIN_SPAN), :], qs)
    dist = (dbase * Q_BLOCK + lax.broadcasted_iota(jnp.int32, (Q_BLOCK, Q_BLOCK), 1)
            - lax.broadcasted_iota(jnp.int32, (Q_BLOCK, Q_BLOCK), 0))
    too_old = per_head(jnp.where(dist <= WIN - 1, 0.0, NEG_INF))
    n_full, rem = divmod(WIN_SPAN, KEY_TILE)
    pieces = [bias_tile(dbase - (KEY_TILE // Q_BLOCK) * u) for u in range(n_full)]
    if rem:
        pieces.append(bias_tile(dbase - (KEY_TILE // Q_BLOCK) * n_full)[0:rem])
    pieces[0] = jnp.concatenate([pieces[0][0:Q_BLOCK] + too_old, pieces[0][Q_BLOCK:]], axis=0)
    sc = sc + jnp.concatenate(pieces, axis=0)
    m = jnp.max(sc, axis=0, keepdims=True)
    e = jnp.exp(sc - m)
    l_win = jnp.sum(e, axis=0, keepdims=True)
    vt = jnp.concatenate([vwt_ref[j0 + u] for u in range(WIN_SPAN // Q_BLOCK)], axis=1)
    o_win = _dot(vt, e.astype(BF16)) / jnp.maximum(l_win, TINY)

    gates_t = jnp.transpose(_sigmoid(bg_ref[...].astype(F32)))

    def gate_row(branch):
        return jnp.concatenate(
            [gates_t[3 * r + branch:3 * r + branch + 1, :] for r in range(NSA_REP)], axis=1)

    o_t = gate_row(0) * o_cmp + gate_row(1) * o_sel + gate_row(2) * o_win
    for r in range(NSA_REP):
        o = jnp.transpose(o_t[:, r * Q_BLOCK:(r + 1) * Q_BLOCK])
        gp = gp_ref[:, r * dh:(r + 1) * dh].astype(F32)
        y_ref[:, r * dh:(r + 1) * dh] = (o * _silu(gp)).astype(BF16)


def _nsa_layout(s):
    nch = s // CMP_STRIDE
    nc = (s - CMP_BLOCK) // CMP_STRIDE + 1
    nsel = s // SEL_BLOCK
    start = np.arange(nc) * CMP_STRIDE
    sel_start = np.arange(nsel) * SEL_BLOCK
    cover = (start[:, None] < sel_start[None, :] + SEL_BLOCK) & (
        start[:, None] + CMP_BLOCK - 1 >= sel_start[None, :])
    cov_t = np.zeros((LANES, nch), np.float32)
    cov_t[:nsel, :nc] = cover.T
    key_blk = np.arange(s) // SEL_BLOCK
    expand = (key_blk[:, None] == np.arange(LANES)[None, :]).astype(np.float32)
    expand = expand.reshape(s // SEL_TILE, SEL_TILE, LANES)
    return jnp.asarray(cov_t, BF16), jnp.asarray(expand, BF16)


def _nsa_attention(proj, cmp_kv, cmp_t, vs_t, vw_t, bias, bn, s):
    t, ncols = proj.shape
    g_n, dh, rep = NSA_KV_GROUPS, NSA_HEAD_DIM, NSA_REP
    gw = rep * dh
    nq = s // Q_BLOCK
    nch = s // CMP_STRIDE
    nsel = s // SEL_BLOCK
    nd = bias.shape[1] - 1
    cov_t, expand = _nsa_layout(s)
    kv0 = NSA_WIDTH // dh
    kcol = lambda idx: pl.BlockSpec((s, dh), lambda b, g, i: (b, kv0 + idx * g_n + g))
    vtile = pl.BlockSpec((None, nq, dh, Q_BLOCK), lambda b, g, i: (b * g_n + g, 0, 0, 0))
    gp0 = (NSA_WIDTH + 6 * NSA_KV_WIDTH) // gw
    bg0 = (2 * NSA_WIDTH + 6 * NSA_KV_WIDTH) // LANES
    kern = functools.partial(_nsa_kernel, nsel=nsel, topk=min(SEL_TOPK, nsel), nd=nd)
    return pl.pallas_call(
        kern,
        grid=(bn, g_n, nq),
        in_specs=[
            pl.BlockSpec((Q_BLOCK, gw), lambda b, g, i: (b * nq + i, g)),
            pl.BlockSpec((None, nch, dh), lambda b, g, i: (0, b * g_n + g, 0)),
            pl.BlockSpec((None, None, dh, nch), lambda b, g, i: (1, b * g_n + g, 0, 0)),
            kcol(2), vtile, kcol(4), vtile,
            pl.BlockSpec((Q_BLOCK, gw), lambda b, g, i: (b * nq + i, gp0 + g)),
            pl.BlockSpec((Q_BLOCK, LANES), lambda b, g, i: (b * nq + i, bg0 + g)),
            pl.BlockSpec((None, nd + 1, KEY_TILE, rep * Q_BLOCK), lambda b, g, i: (g, 0, 0, 0),
                         pipeline_mode=pl.Buffered(1)),
            _resident(cov_t.shape, lambda b, g, i: (0, 0)),
            _resident(expand.shape, lambda b, g, i: (0, 0, 0)),
        ],
        out_specs=pl.BlockSpec((Q_BLOCK, gw), lambda b, g, i: (b * nq + i, g)),
        out_shape=jax.ShapeDtypeStruct((t, NSA_WIDTH), BF16),
        scratch_shapes=[pltpu.VMEM((dh, rep * Q_BLOCK), F32)],
        compiler_params=_params("parallel", "parallel", "arbitrary"),
        name="nsa_attention",
    )(proj, cmp_kv, cmp_t, proj, vs_t, proj, vw_t, proj, proj, bias, cov_t, expand)


def _nsa_in_weight(w_in):
    main = NSA_WIDTH + 6 * NSA_KV_WIDTH
    ngate = 3 * NSA_HEADS
    per_group = ngate // NSA_KV_GROUPS
    d = w_in.shape[0]
    bg = w_in[:, main:main + ngate].reshape(d, NSA_KV_GROUPS, per_group)
    bg = jnp.pad(bg, ((0, 0), (0, 0), (0, LANES - per_group))).reshape(d, NSA_KV_GROUPS * LANES)
    return jnp.concatenate([w_in[:, :main], w_in[:, main + ngate:], bg], axis=1).astype(BF16)


def _nsa_mixer(h, g_pre, bn, s, w_in, pos_k, w1_k, w2_k, pos_v, w1_v, w2_v, rel_bias):
    t = h.shape[0]
    g_n, dh = NSA_KV_GROUPS, NSA_HEAD_DIM
    proj = _norm_matmul(h, g_pre, _nsa_in_weight(w_in), 512)
    nch = s // CMP_STRIDE
    p3 = proj.reshape(bn, nch, CMP_STRIDE, proj.shape[1])

    def chunks(c0):
        x = p3[..., c0:c0 + NSA_KV_WIDTH].reshape(bn, nch, CMP_STRIDE, g_n, dh)
        return jnp.transpose(x, (0, 3, 1, 2, 4)).reshape(bn * g_n * nch, CMP_STRIDE * dh)

    xkv = jnp.stack([chunks(NSA_WIDTH), chunks(NSA_WIDTH + NSA_KV_WIDTH)])
    half = CMP_STRIDE * dh
    pos = jnp.stack([pos_k.reshape(2, half), pos_v.reshape(2, half)])
    w1 = jnp.stack([w1_k.reshape(2, half, -1), w1_v.reshape(2, half, -1)]).astype(BF16)
    w2 = jnp.stack([w2_k, w2_v]).astype(BF16)
    cmp_kv, cmp_t = _compress(xkv, pos, w1, w2, nch)

    def value_tiles_t(c0):
        x = proj.reshape(bn, s // Q_BLOCK, Q_BLOCK, proj.shape[1])[..., c0:c0 + NSA_KV_WIDTH]
        x = x.reshape(bn, s // Q_BLOCK, Q_BLOCK, g_n, dh)
        return jnp.transpose(x, (0, 3, 1, 4, 2)).reshape(bn * g_n, s // Q_BLOCK, dh, Q_BLOCK)

    vs_t = value_tiles_t(NSA_WIDTH + 3 * NSA_KV_WIDTH)
    vw_t = value_tiles_t(NSA_WIDTH + 5 * NSA_KV_WIDTH)
    nd = min(s // Q_BLOCK, 15)
    bias = _causal_bias(rel_bias, nd)
    return _nsa_attention(proj, cmp_kv, cmp_t, vs_t, vw_t, bias, bn, s)


def kernel(x, p, rel_bias, norm_pre, norm_post, ple_w_proj, ple_w_gate, a_w_in, a_conv_w, a_conv_b,
           a_w_r, a_b_r, a_w_i, a_b_i, a_lam, a_w_out, b_w_in, b_w_out, c_w_in, c_cmp_pos_k,
           c_cmp_w1_k, c_cmp_w2_k, c_cmp_pos_v, c_cmp_w1_v, c_cmp_w2_v, c_w_out):
    bn, s, d = x.shape
    depth = p.shape[0]
    t = bn * s
    h = x.reshape(t, d)
    for i in range(depth):
        kind = i % N_MIXERS
        j = i // N_MIXERS
        if kind == 0:
            proj = _norm_matmul(h, norm_pre[i], a_w_in[j].astype(BF16), 512)
            y = _rglru(proj, bn, s, a_conv_w[j], a_conv_b[j], a_w_r[j], a_b_r[j], a_w_i[j],
                       a_b_i[j], a_lam[j])
            w_out = a_w_out[j]
        elif kind == 1:
            y = _dilated_mixer(h, norm_pre[i], bn, s, b_w_in[j], rel_bias)
            w_out = b_w_out[j]
        else:
            y = _nsa_mixer(h, norm_pre[i], bn, s, c_w_in[j], c_cmp_pos_k[j], c_cmp_w1_k[j],
                           c_cmp_w2_k[j], c_cmp_pos_v[j], c_cmp_w1_v[j], c_cmp_w2_v[j], rel_bias)
            w_out = c_w_out[j]
        h = _out_ple(y, w_out.astype(BF16), norm_post[i], h, p[i].reshape(t, -1),
                     ple_w_proj[i].astype(BF16), ple_w_gate[i].astype(BF16))
    return h.reshape(bn, s, d)
```

```python
import functools
import math

import numpy as np
import jax
import jax.numpy as jnp
from jax import lax
from jax.experimental import pallas as pl
from jax.experimental.pallas import tpu as pltpu

F32 = jnp.float32
BF16 = jnp.bfloat16

NORM_EPS = 1e-6
NEG_INF = -1e30
TINY = 1e-30
MASK_BIG = 2.0 ** 40

N_MIXERS = 3
N_BUCKETS = 32
BUCKET_MAX_DIST = 2048

LRU_BLOCKS = 8
CONV_W = 4
LRU_C = 8.0

DIL_PATTERNS = ((128, 1), (512, 4), (2048, 16))
DIL_HEADS = 16
DIL_HEAD_DIM = 64
DIL_WIDTH = DIL_HEADS * DIL_HEAD_DIM

NSA_HEADS = 16
NSA_KV_GROUPS = 4
NSA_REP = NSA_HEADS // NSA_KV_GROUPS
NSA_HEAD_DIM = 128
NSA_WIDTH = NSA_HEADS * NSA_HEAD_DIM
NSA_KV_WIDTH = NSA_KV_GROUPS * NSA_HEAD_DIM
CMP_BLOCK = 32
CMP_STRIDE = 16
SEL_BLOCK = 64
SEL_TOPK = 16
WIN = 512
FORCE_BONUS = 1e4

SUBLANES = 8
LANES = 128
Q_BLOCK = 128
KEY_TILE = 256
DIL_WINDOW = Q_BLOCK * max(d for _, d in DIL_PATTERNS)
SEL_TILE = 1024
PROJ_ROWS = 512
PROJ_HALVES = 2
WIN_SPAN = WIN + Q_BLOCK
VMEM_LIMIT_BYTES = 56 * 1024 * 1024


def _params(*sem):
    return pltpu.CompilerParams(dimension_semantics=sem, vmem_limit_bytes=VMEM_LIMIT_BYTES)


def _dot(a, b):
    return jnp.dot(a, b, preferred_element_type=F32)


def _dot_nt(a, b):
    return lax.dot_general(a, b, (((1,), (1,)), ((), ())), preferred_element_type=F32)


def _sigmoid(x):
    return 1.0 / (1.0 + jnp.exp(-x))


def _silu(x):
    return x * _sigmoid(x)


def _resident(block_shape, index_map):
    return pl.BlockSpec(block_shape, index_map, pipeline_mode=pl.Buffered(1))


def _norm_matmul_kernel(h_ref, g_ref, w_ref, o_ref, *scratch, dil):
    n = w_ref.shape[1]
    half_rows = h_ref.shape[0] // PROJ_HALVES
    for hh in range(PROJ_HALVES):
        x = h_ref[hh * half_rows:(hh + 1) * half_rows, :]
        ms = jnp.mean(x * x, axis=-1, keepdims=True)
        u = (x * lax.rsqrt(ms + NORM_EPS) * g_ref[...]).astype(BF16)
        res = _dot(u, w_ref[...])
        if dil == 1:
            o_ref[hh * half_rows:(hh + 1) * half_rows, :] = res.astype(o_ref.dtype)
            continue
        res_scr = scratch[0]
        per = half_rows // dil
        for c in range(n // LANES):
            res_scr[hh, c] = res[:, c * LANES:(c + 1) * LANES]
        for r in range(dil):
            for c in range(n // LANES):
                o_ref[r, hh * per:(hh + 1) * per, c * LANES:(c + 1) * LANES] = (
                    res_scr[hh, c, pl.ds(r, per, stride=dil), :].astype(o_ref.dtype))


def _norm_matmul(h, g, w, dil=1, bn=1):
    t, d = h.shape
    n = w.shape[1]
    tm = min(t, PROJ_ROWS)
    in_specs = [
        pl.BlockSpec((tm, d), lambda i: (i, 0)),
        _resident((1, d), lambda i: (0, 0)),
        _resident((d, n), lambda i: (0, 0)),
    ]
    if dil > 1:
        s = t // bn
        nt = s // tm
        out_spec = pl.BlockSpec((None, dil, tm // dil, n), lambda i: (i // nt, 0, i % nt, 0))
        out_shape = jax.ShapeDtypeStruct((bn, dil, s // dil, n), BF16)
        scratch = [pltpu.VMEM((PROJ_HALVES, n // LANES, tm // PROJ_HALVES, LANES), F32)]
    else:
        out_spec = pl.BlockSpec((tm, n), lambda i: (i, 0))
        out_shape = jax.ShapeDtypeStruct((t, n), BF16)
        scratch = []
    return pl.pallas_call(
        functools.partial(_norm_matmul_kernel, dil=dil),
        grid=(t // tm,),
        in_specs=in_specs,
        out_specs=out_spec,
        out_shape=out_shape,
        scratch_shapes=scratch,
        compiler_params=_params("parallel"),
        name="norm_matmul" if dil == 1 else f"norm_matmul_residue_{dil}",
    )(h, g.reshape(1, d), w)


def _out_ple_kernel(y_ref, wo_ref, g_ref, h_ref, p_ref, wp_ref, wg_ref, o_ref):
    z = _dot(y_ref[...], wo_ref[...])
    ms = jnp.mean(z * z, axis=-1, keepdims=True)
    h1 = h_ref[...] + z * lax.rsqrt(ms + NORM_EPS) * g_ref[...]
    gate = _sigmoid(_dot(h1.astype(BF16), wg_ref[...]))
    pe = _dot(p_ref[...].astype(BF16), wp_ref[...])
    o_ref[...] = h1 + gate * pe


def _out_ple(y, w_out, g_post, h, p, w_proj, w_gate):
    t, d = h.shape
    k = y.shape[1]
    pd = p.shape[1]
    tm = min(t, 256)
    return pl.pallas_call(
        _out_ple_kernel,
        grid=(t // tm,),
        in_specs=[
            pl.BlockSpec((tm, k), lambda i: (i, 0)),
            _resident((k, d), lambda i: (0, 0)),
            _resident((1, d), lambda i: (0, 0)),
            pl.BlockSpec((tm, d), lambda i: (i, 0)),
            pl.BlockSpec((tm, pd), lambda i: (i, 0)),
            _resident((pd, d), lambda i: (0, 0)),
            _resident((d, d), lambda i: (0, 0)),
        ],
        out_specs=pl.BlockSpec((tm, d), lambda i: (i, 0)),
        out_shape=jax.ShapeDtypeStruct((t, d), F32),
        compiler_params=_params("parallel"),
        name="out_ple",
    )(y, w_out, g_post.reshape(1, d), h, p, w_proj, w_gate)


def _lru_kernel(xb_ref, gate_ref, cw_ref, cb_ref, wr_ref, br_ref, wi_ref, bi_ref, lam_ref,
                y_ref, xext_scr, h_scr, *, ts, bw):
    @pl.when(pl.program_id(1) == 0)
    def _():
        xext_scr[0:SUBLANES, :] = jnp.zeros((SUBLANES, xext_scr.shape[1]), F32)
        h_scr[...] = jnp.zeros(h_scr.shape, F32)

    row = lax.broadcasted_iota(jnp.int32, (SUBLANES, bw), 0)
    for n in range(LRU_BLOCKS):
        cols = slice(n * bw, (n + 1) * bw)
        x = xb_ref[:, cols].astype(F32)
        xext_scr[SUBLANES:SUBLANES + ts, cols] = x
        xc = cw_ref[3:4, cols] * x + cb_ref[:, cols]
        for k in range(1, CONV_W):
            xc = xc + cw_ref[3 - k:4 - k, cols] * xext_scr[SUBLANES - k:SUBLANES - k + ts, cols]
        xext_scr[0:SUBLANES, cols] = x[ts - SUBLANES:ts, :]

        xn = xc.astype(BF16)
        r = _sigmoid(_dot(xn, wr_ref[n]) + br_ref[:, cols])
        ig = _sigmoid(_dot(xn, wi_ref[n]) + bi_ref[:, cols])
        nl = -lam_ref[:, cols]
        softplus = jnp.maximum(nl, 0.0) + jnp.log(1.0 + jnp.exp(-jnp.abs(nl)))
        a = jnp.exp(-LRU_C * r * softplus)
        b = jnp.sqrt(1.0 - a * a) * (ig * xc)

        h = h_scr[:, cols]
        outs = []
        for grp in range(ts // SUBLANES):
            a8 = a[grp * SUBLANES:(grp + 1) * SUBLANES, :]
            b8 = b[grp * SUBLANES:(grp + 1) * SUBLANES, :]
            for dist in (1, 2, 4):
                a_sh = jnp.where(row >= dist, pltpu.roll(a8, dist, axis=0), 1.0)
                b_sh = jnp.where(row >= dist, pltpu.roll(b8, dist, axis=0), 0.0)
                b8 = a8 * b_sh + b8
                a8 = a8 * a_sh
            hh = a8 * h + b8
            h = jnp.broadcast_to(hh[SUBLANES - 1:SUBLANES, :], (SUBLANES, bw))
            outs.append(hh)
        h_scr[:, cols] = h
        hs = jnp.concatenate(outs, axis=0)
        y_ref[:, cols] = (hs * _silu(gate_ref[:, cols].astype(F32))).astype(BF16)


def _rglru(proj, bn, s, conv_w, conv_b, w_r, b_r, w_i, b_i, lam):
    t, two_c = proj.shape
    c = two_c // 2
    bw = c // LRU_BLOCKS
    ts = min(s, 256)
    nst = s // ts
    row = lambda v: v.reshape(1, c)
    kern = functools.partial(_lru_kernel, ts=ts, bw=bw)
    return pl.pallas_call(
        kern,
        grid=(bn, nst),
        in_specs=[
            pl.BlockSpec((ts, c), lambda b, i: (b * nst + i, 0)),
            pl.BlockSpec((ts, c), lambda b, i: (b * nst + i, 1)),
            pl.BlockSpec((CONV_W, c), lambda b, i: (0, 0)),
            pl.BlockSpec((1, c), lambda b, i: (0, 0)),
            pl.BlockSpec((LRU_BLOCKS, bw, bw), lambda b, i: (0, 0, 0)),
            pl.BlockSpec((1, c), lambda b, i: (0, 0)),
            pl.BlockSpec((LRU_BLOCKS, bw, bw), lambda b, i: (0, 0, 0)),
            pl.BlockSpec((1, c), lambda b, i: (0, 0)),
            pl.BlockSpec((1, c), lambda b, i: (0, 0)),
        ],
        out_specs=pl.BlockSpec((ts, c), lambda b, i: (b * nst + i, 0)),
        out_shape=jax.ShapeDtypeStruct((t, c), BF16),
        scratch_shapes=[pltpu.VMEM((ts + SUBLANES, c), F32), pltpu.VMEM((SUBLANES, c), F32)],
        compiler_params=_params("parallel", "arbitrary"),
        name="rglru",
    )(proj, proj, conv_w, row(conv_b), w_r.astype(BF16), row(b_r), w_i.astype(BF16), row(b_i),
      row(lam))


def _t5_bucket_np(dist):
    n = np.maximum(dist, 0)
    max_exact = N_BUCKETS // 2
    nf = np.maximum(n, max_exact).astype(np.float64)
    large = max_exact + (np.log(nf / max_exact) / math.log(BUCKET_MAX_DIST / max_exact)
                         * (N_BUCKETS - max_exact)).astype(np.int64)
    return np.where(n < max_exact, n, np.minimum(large, N_BUCKETS - 1)).astype(np.int32)


TOEPLITZ_PERIOD = Q_BLOCK + KEY_TILE


def _toeplitz(w, rows, cols):
    period = w.shape[-1]
    reps = -(-rows * (period - 1) // period)
    flat = jnp.tile(w, (1,) * (w.ndim - 1) + (reps,))[..., :rows * (period - 1)]
    return flat.reshape(w.shape[:-1] + (rows, period - 1))[..., :cols]


def _signed_offsets(n_cols):
    m = np.arange(TOEPLITZ_PERIOD)
    return np.where(m < n_cols, m, m - TOEPLITZ_PERIOD)


def _dilated_bias(rel_bias, max_delta, dilation):
    delta = Q_BLOCK - _signed_offsets(2 * Q_BLOCK)
    ok = (delta >= 0) & (delta <= max_delta)
    vec = jnp.where(jnp.asarray(ok)[:, None], rel_bias[_t5_bucket_np(delta * dilation)], NEG_INF)
    return _toeplitz(jnp.transpose(vec).astype(F32), Q_BLOCK, 2 * Q_BLOCK)


def _causal_bias(rel_bias, n_dist):
    heads = rel_bias.shape[1]
    dist = np.arange(n_dist)[:, None] * Q_BLOCK + _signed_offsets(Q_BLOCK)[None, :]
    vec = jnp.where(jnp.asarray(dist >= 0)[..., None], rel_bias[_t5_bucket_np(dist)], NEG_INF)
    vec = jnp.transpose(vec, (0, 2, 1)).astype(F32)
    vec = jnp.concatenate([vec, jnp.full((1, heads, TOEPLITZ_PERIOD), NEG_INF, F32)])
    tbl = _toeplitz(vec, KEY_TILE, Q_BLOCK)
    tbl = tbl.reshape(n_dist + 1, NSA_KV_GROUPS, NSA_REP, KEY_TILE, Q_BLOCK)
    tbl = jnp.transpose(tbl, (1, 0, 3, 2, 4))
    return tbl.reshape(NSA_KV_GROUPS, n_dist + 1, KEY_TILE, NSA_REP * Q_BLOCK)


def _dil_kernel(*refs, dils):
    ng = len(dils)
    groups = [refs[5 * g:5 * g + 5] for g in range(ng)]
    bias_refs = refs[5 * ng:6 * ng]
    gate_ref, y_ref = refs[6 * ng], refs[6 * ng + 1]
    ext = refs[6 * ng + 2:8 * ng + 2]
    o_acc, lse_acc = refs[8 * ng + 2:]
    first = pl.program_id(1) == 0
    scale = DIL_HEAD_DIM ** -0.5
    lane = lax.broadcasted_iota(jnp.int32, (Q_BLOCK, LANES), 1)
    low = lane < DIL_HEAD_DIM
    col = lax.broadcasted_iota(jnp.int32, (Q_BLOCK, 2 * Q_BLOCK), 1)
    halo_pen = jnp.where(col < Q_BLOCK, jnp.where(first, NEG_INF, 0.0), 0.0)
    for g, dil in enumerate(dils):
        q_ref, k_ref, kh_ref, v_ref, vh_ref = groups[g]
        kext, vext = ext[2 * g], ext[2 * g + 1]
        kext[:, 0:Q_BLOCK, :] = kh_ref[...]
        kext[:, Q_BLOCK:, :] = k_ref[...]
        vext[:, 0:Q_BLOCK, :] = vh_ref[...]
        vext[:, Q_BLOCK:, :] = v_ref[...]
        for r in range(dil):
            for j in range(k_ref.shape[1] // Q_BLOCK):
                qp = q_ref[r, j * Q_BLOCK:(j + 1) * Q_BLOCK, :]
                kb = kext[r, j * Q_BLOCK:(j + 2) * Q_BLOCK, :]
                vb = vext[r, j * Q_BLOCK:(j + 2) * Q_BLOCK, :]
                o_half, lse_half = [], []
                for half in range(2):
                    qm = jnp.where(low if half == 0 else jnp.logical_not(low), qp, jnp.zeros_like(qp))
                    sc = _dot_nt(qm, kb) * scale + bias_refs[g][half]
                    if j == 0:
                        sc = sc + halo_pen
                    m = jnp.max(sc, axis=-1, keepdims=True)
                    e = jnp.exp(sc - m)
                    l = jnp.sum(e, axis=-1, keepdims=True)
                    pv = _dot(e.astype(BF16), vb)
                    o_half.append(pv / l)
                    lse_half.append(jnp.broadcast_to(m + jnp.log(l), (Q_BLOCK, LANES)))
                o_new = jnp.where(low, o_half[0], o_half[1])
                lse_new = jnp.where(low, lse_half[0], lse_half[1])
                if dil == 1:
                    rows = pl.ds(j * Q_BLOCK, Q_BLOCK)
                else:
                    rows = pl.ds(r + dil * j * Q_BLOCK, Q_BLOCK, stride=dil)
                if g == 0:
                    o_acc[rows, :] = o_new
                    lse_acc[rows, :] = lse_new
                else:
                    o_old = o_acc[rows, :]
                    lse_old = lse_acc[rows, :]
                    top = jnp.maximum(lse_old, lse_new)
                    w_old = jnp.exp(lse_old - top)
                    w_new = jnp.exp(lse_new - top)
                    tot = w_old + w_new
                    o_acc[rows, :] = (w_old * o_old + w_new * o_new) / tot
                    if g + 1 < ng:
                        lse_acc[rows, :] = top + jnp.log(tot)
    y_ref[...] = (o_acc[...] * _silu(gate_ref[...].astype(F32))).astype(BF16)


def _dil_attention(arrays, biases, bn, s):
    dils = tuple(d for _, d in DIL_PATTERNS)
    win = DIL_WINDOW
    pairs = DIL_WIDTH // LANES
    nw = s // win
    in_specs, operands, scratch = [], [], []
    for g, dil in enumerate(dils):
        per = win // dil
        per_blocks = per // Q_BLOCK
        cur = lambda cb, dil=dil, per=per: pl.BlockSpec(
            (None, dil, per, LANES), lambda b, w, p: (b, 0, w, cb * pairs + p))
        halo = lambda cb, dil=dil, per_blocks=per_blocks: pl.BlockSpec(
            (None, dil, Q_BLOCK, LANES),
            lambda b, w, p: (b, 0, jnp.maximum(w * per_blocks - 1, 0), cb * pairs + p))
        in_specs += [cur(0), cur(1), halo(1), cur(2), halo(2)]
        operands += [arrays[g]] * 5
        scratch += [pltpu.VMEM((dil, per + Q_BLOCK, LANES), BF16)] * 2
    in_specs += [pl.BlockSpec((2, Q_BLOCK, 2 * Q_BLOCK), lambda b, w, p: (p, 0, 0))] * len(dils)
    operands += list(biases)
    in_specs.append(pl.BlockSpec((None, None, win, LANES), lambda b, w, p: (b, 0, w, 3 * pairs + p)))
    operands.append(arrays[0])
    scratch += [pltpu.VMEM((win, LANES), F32)] * 2
    return pl.pallas_call(
        functools.partial(_dil_kernel, dils=dils),
        grid=(bn, nw, pairs),
        in_specs=in_specs,
        out_specs=pl.BlockSpec((win, LANES), lambda b, w, p: (b * nw + w, p)),
        out_shape=jax.ShapeDtypeStruct((bn * s, DIL_WIDTH), BF16),
        scratch_shapes=scratch,
        compiler_params=_params("parallel", "arbitrary", "arbitrary"),
        name="dilated_attention",
    )(*operands)


def _dilated_mixer(h, g_pre, bn, s, w_in, rel_bias):
    w = DIL_WIDTH
    ng = len(DIL_PATTERNS)
    wb = w_in.astype(BF16)
    main_w = jnp.concatenate([wb[:, :3 * w], wb[:, 3 * ng * w:]], axis=1)
    arrays = [_norm_matmul(h, g_pre, main_w).reshape(bn, 1, s, 4 * w)]
    for gi in range(1, ng):
        dil = DIL_PATTERNS[gi][1]
        arrays.append(_norm_matmul(h, g_pre, wb[:, 3 * gi * w:3 * (gi + 1) * w], dil=dil, bn=bn))
    biases = [_dilated_bias(rel_bias, window // dil, dil) for window, dil in DIL_PATTERNS]
    return _dil_attention(arrays, biases, bn, s)


def _gelu_tanh(x):
    return 0.5 * x * (1.0 + jnp.tanh(math.sqrt(2.0 / math.pi) * (x + 0.044715 * (x * x * x))))


def _compress_kernel(x_ref, pos_ref, w1_ref, w2_ref, o_ref, ot_ref, *, nch):
    x = x_ref[...].astype(F32)
    top = _dot((x + pos_ref[0:1, :]).astype(BF16), w1_ref[0])
    bot = _dot((x + pos_ref[1:2, :]).astype(BF16), w1_ref[1])
    hid = top + pltpu.roll(bot, nch - 1, axis=0)
    out = _dot(_gelu_tanh(hid).astype(BF16), w2_ref[...])
    o_ref[...] = out.astype(BF16)
    ot_ref[...] = jnp.transpose(out).astype(BF16)


def _compress(xkv, pos, w1, w2, nch):
    _, rows, width = xkv.shape
    hidden = w1.shape[-1]
    dh = w2.shape[-1]
    return pl.pallas_call(
        functools.partial(_compress_kernel, nch=nch),
        grid=(2, rows // nch),
        in_specs=[
            pl.BlockSpec((None, nch, width), lambda kv, i: (kv, i, 0)),
            pl.BlockSpec((None, 2, width), lambda kv, i: (kv, 0, 0)),
            pl.BlockSpec((None, 2, width, hidden), lambda kv, i: (kv, 0, 0, 0)),
            pl.BlockSpec((None, hidden, dh), lambda kv, i: (kv, 0, 0)),
        ],
        out_specs=[pl.BlockSpec((None, nch, dh), lambda kv, i: (kv, i, 0)),
                   pl.BlockSpec((None, None, dh, nch), lambda kv, i: (kv, i, 0, 0))],
        out_shape=[jax.ShapeDtypeStruct((2, rows, dh), BF16),
                   jax.ShapeDtypeStruct((2, rows // nch, dh, nch), BF16)],
        compiler_params=_params("parallel", "parallel"),
        name="nsa_compress",
    )(xkv, pos, w1, w2)


def _nsa_kernel(q_ref, kc_ref, vct_ref, ks_ref, vst_ref, kw_ref, vwt_ref, gp_ref, bg_ref,
                bias_ref, cov_ref, exp_ref, y_ref, acc_scr, *, nsel, topk, nd):
    i = pl.program_id(2)
    t0 = i * Q_BLOCK
    dh = NSA_HEAD_DIM
    rq = NSA_REP * Q_BLOCK
    scale = dh ** -0.5
    q_all = jnp.concatenate([q_ref[:, r * dh:(r + 1) * dh] for r in range(NSA_REP)], axis=0)
    qs = (q_all.astype(F32) * scale).astype(BF16)

    def bias_tile(d):
        return bias_ref[jnp.where(d < 0, nd, jnp.minimum(d, nd - 1))]

    def per_head(x):
        return jnp.concatenate([x] * NSA_REP, axis=1)

    kc = kc_ref[...]
    nch = kc.shape[0]
    t_q = t0 + (lax.broadcasted_iota(jnp.int32, (nch, rq), 1) & (Q_BLOCK - 1))
    cend = lax.broadcasted_iota(jnp.int32, (nch, rq), 0) * CMP_STRIDE + (CMP_BLOCK - 1)
    cmask = cend <= t_q
    sc = jnp.where(cmask, _dot_nt(kc, qs), NEG_INF)
    m = jnp.max(sc, axis=0, keepdims=True)
    e = jnp.where(cmask, jnp.exp(sc - m), 0.0)
    prob = e / jnp.maximum(jnp.sum(e, axis=0, keepdims=True), TINY)
    o_cmp = _dot(vct_ref[...], prob.astype(BF16))
    psum = prob[:, 0:Q_BLOCK]
    for r in range(1, NSA_REP):
        psum = psum + prob[:, r * Q_BLOCK:(r + 1) * Q_BLOCK]
    p_hi = psum.astype(BF16)
    p_lo = (psum - p_hi.astype(F32)).astype(BF16)
    cov = cov_ref[...]
    imp = (_dot(cov, p_hi) + _dot(cov, p_lo))[0:nsel, :]
    blk = lax.broadcasted_iota(jnp.int32, (nsel, Q_BLOCK), 0)
    t_col = t0 + lax.broadcasted_iota(jnp.int32, (nsel, Q_BLOCK), 1)
    cur = t_col // SEL_BLOCK
    forced = (blk == 0) | (blk == cur) | (blk == cur - 1)
    valid = blk * SEL_BLOCK <= t_col
    imp = jnp.where(valid, imp + jnp.where(forced, FORCE_BONUS, 0.0), NEG_INF)
    ngrp = nsel // SUBLANES
    rank = [jnp.zeros((SUBLANES, Q_BLOCK), jnp.int32) for _ in range(ngrp)]
    parts = [imp[g * SUBLANES:(g + 1) * SUBLANES, :] for g in range(ngrp)]
    sub = lax.broadcasted_iota(jnp.int32, (SUBLANES, Q_BLOCK), 0)
    for mrow in range(nsel):
        other = jnp.broadcast_to(imp[mrow:mrow + 1, :], (SUBLANES, Q_BLOCK))
        for g in range(ngrp):
            ge = jnp.where(other >= parts[g], 1, 0)
            gt = jnp.where(other > parts[g], 1, 0)
            if g * SUBLANES > mrow:
                before = ge
            elif (g + 1) * SUBLANES - 1 < mrow:
                before = gt
            else:
                before = jnp.where(sub + g * SUBLANES > mrow, ge, gt)
            rank[g] = rank[g] + before
    rank = jnp.concatenate(rank, axis=0)
    sel_t = jnp.where(valid & (rank < topk), 1.0, 0.0)
    if nsel < LANES:
        sel_t = jnp.concatenate([sel_t, jnp.zeros((LANES - nsel, Q_BLOCK), F32)], axis=0)

    j0 = jnp.maximum(i - WIN // Q_BLOCK, 0)
    start = pl.multiple_of(j0 * Q_BLOCK, Q_BLOCK)
    dbase = i - j0
    sc = _dot_nt(kw_ref[pl.ds(start, WIN_SPAN), :], qs)
    dist = (dbase * Q_BLOCK + lax.broadcasted_iota(jnp.int32, (Q_BLOCK, Q_BLOCK), 1)
            - lax.broadcasted_iota(jnp.int32, (Q_BLOCK, Q_BLOCK), 0))
    too_old = per_head(jnp.where(dist <= WIN - 1, 0.0, NEG_INF))
    n_full, rem = divmod(WIN_SPAN, KEY_TILE)
    pieces = [bias_tile(dbase - (KEY_TILE // Q_BLOCK) * u) for u in range(n_full)]
    if rem:
        pieces.append(bias_tile(dbase - (KEY_TILE // Q_BLOCK) * n_full)[0:rem])
    pieces[0] = jnp.concatenate([pieces[0][0:Q_BLOCK] + too_old, pieces[0][Q_BLOCK:]], axis=0)
    sc = sc + jnp.concatenate(pieces, axis=0)
    m = jnp.max(sc, axis=0, keepdims=True)
    e = jnp.exp(sc - m)
    l_win = jnp.sum(e, axis=0, keepdims=True)
    vt = jnp.concatenate([vwt_ref[j0 + u] for u in range(WIN_SPAN // Q_BLOCK)], axis=1)
    o_win = _dot(vt, e.astype(BF16)) / jnp.maximum(l_win, TINY)

    sub_tiles = SEL_TILE // Q_BLOCK
    sel_pen = jnp.transpose(jnp.where(sel_t > 0.5, 0.0, -MASK_BIG)).astype(BF16)
    q_aug = jnp.concatenate([qs, jnp.concatenate([sel_pen] * NSA_REP, axis=0)], axis=1)
    acc_scr[...] = jnp.zeros(acc_scr.shape, F32)

    def sel_body(kt, carry):
        m_prev, l_prev = carry
        k0 = pl.multiple_of(kt * SEL_TILE, SEL_TILE)
        k_aug = jnp.concatenate([ks_ref[pl.ds(k0, SEL_TILE), :], exp_ref[kt]], axis=1)
        d0 = i - sub_tiles * kt
        bias = jnp.concatenate(
            [bias_tile(d0 - (KEY_TILE // Q_BLOCK) * u) for u in range(SEL_TILE // KEY_TILE)], axis=0)
        sc = _dot_nt(k_aug, q_aug) + bias
        m_new = jnp.maximum(m_prev, jnp.max(sc, axis=0, keepdims=True))
        alpha = jnp.exp(m_prev - m_new)
        e = jnp.exp(sc - m_new)
        l_new = alpha * l_prev + jnp.sum(e, axis=0, keepdims=True)
        vt = jnp.concatenate([vst_ref[kt * sub_tiles + u] for u in range(sub_tiles)], axis=1)
        acc_scr[...] = alpha * acc_scr[...] + _dot(vt, e.astype(BF16))
        return m_new, l_new

    init = (jnp.full((1, rq), NEG_INF, F32), jnp.zeros((1, rq), F32))
    _, l_sel = lax.fori_loop(0, i // sub_tiles + 1, sel_body, init)
    o_sel = acc_scr[...] / jnp.maximum(l_sel, TINY)

    gates_t = jnp.transpose(_sigmoid(bg_ref[...].astype(F32)))

    def gate_row(branch):
        return jnp.concatenate(
            [gates_t[3 * r + branch:3 * r + branch + 1, :] for r in range(NSA_REP)], axis=1)

    o_t = gate_row(0) * o_cmp + gate_row(1) * o_sel + gate_row(2) * o_win
    for r in range(NSA_REP):
        o = jnp.transpose(o_t[:, r * Q_BLOCK:(r + 1) * Q_BLOCK])
        gp = gp_ref[:, r * dh:(r + 1) * dh].astype(F32)
        y_ref[:, r * dh:(r + 1) * dh] = (o * _silu(gp)).astype(BF16)


def _nsa_layout(s):
    nch = s // CMP_STRIDE
    nc = (s - CMP_BLOCK) // CMP_STRIDE + 1
    nsel = s // SEL_BLOCK
    start = np.arange(nc) * CMP_STRIDE
    sel_start = np.arange(nsel) * SEL_BLOCK
    cover = (start[:, None] < sel_start[None, :] + SEL_BLOCK) & (
        start[:, None] + CMP_BLOCK - 1 >= sel_start[None, :])
    cov_t = np.zeros((LANES, nch), np.float32)
    cov_t[:nsel, :nc] = cover.T
    key_blk = np.arange(s) // SEL_BLOCK
    expand = (key_blk[:, None] == np.arange(LANES)[None, :]).astype(np.float32)
    expand = expand.reshape(s // SEL_TILE, SEL_TILE, LANES)
    return jnp.asarray(cov_t, BF16), jnp.asarray(expand, BF16)


def _nsa_attention(proj_a, proj_b, cmp_kv, cmp_t, vs_t, vw_t, bias, bn, s):
    t = proj_a.shape[0]
    g_n, dh, rep = NSA_KV_GROUPS, NSA_HEAD_DIM, NSA_REP
    gw = rep * dh
    nq = s // Q_BLOCK
    nch = s // CMP_STRIDE
    nsel = s // SEL_BLOCK
    nd = bias.shape[1] - 1
    cov_t, expand = _nsa_layout(s)
    ksel0 = (NSA_WIDTH + 2 * NSA_KV_WIDTH) // dh
    kwin0 = NSA_KV_WIDTH // dh
    kcol = lambda c0: pl.BlockSpec((s, dh), lambda b, g, i: (b, c0 + g))
    vtile = pl.BlockSpec((None, nq, dh, Q_BLOCK), lambda b, g, i: (b * g_n + g, 0, 0, 0))
    gp0 = 3 * NSA_KV_WIDTH // gw
    bg0 = (3 * NSA_KV_WIDTH + NSA_WIDTH) // LANES
    kern = functools.partial(_nsa_kernel, nsel=nsel, topk=min(SEL_TOPK, nsel), nd=nd)
    return pl.pallas_call(
        kern,
        grid=(bn, g_n, nq),
        in_specs=[
            pl.BlockSpec((Q_BLOCK, gw), lambda b, g, i: (b * nq + i, g)),
            pl.BlockSpec((None, nch, dh), lambda b, g, i: (0, b * g_n + g, 0)),
            pl.BlockSpec((None, None, dh, nch), lambda b, g, i: (1, b * g_n + g, 0, 0)),
            kcol(ksel0), vtile, kcol(kwin0), vtile,
            pl.BlockSpec((Q_BLOCK, gw), lambda b, g, i: (b * nq + i, gp0 + g)),
            pl.BlockSpec((Q_BLOCK, LANES), lambda b, g, i: (b * nq + i, bg0 + g)),
            pl.BlockSpec((None, nd + 1, KEY_TILE, rep * Q_BLOCK), lambda b, g, i: (g, 0, 0, 0),
                         pipeline_mode=pl.Buffered(1)),
            _resident(cov_t.shape, lambda b, g, i: (0, 0)),
            _resident(expand.shape, lambda b, g, i: (0, 0, 0)),
        ],
        out_specs=pl.BlockSpec((Q_BLOCK, gw), lambda b, g, i: (b * nq + i, g)),
        out_shape=jax.ShapeDtypeStruct((t, NSA_WIDTH), BF16),
        scratch_shapes=[pltpu.VMEM((dh, rep * Q_BLOCK), F32)],
        compiler_params=_params("parallel", "parallel", "arbitrary"),
        name="nsa_attention",
    )(proj_a, cmp_kv, cmp_t, proj_a, vs_t, proj_b, vw_t, proj_b, proj_b, bias, cov_t, expand)


def _nsa_in_weights(w_in):
    split = NSA_WIDTH + 3 * NSA_KV_WIDTH
    main = NSA_WIDTH + 6 * NSA_KV_WIDTH
    ngate = 3 * NSA_HEADS
    per_group = ngate // NSA_KV_GROUPS
    d = w_in.shape[0]
    bg = w_in[:, main:main + ngate].reshape(d, NSA_KV_GROUPS, per_group)
    bg = jnp.pad(bg, ((0, 0), (0, 0), (0, LANES - per_group))).reshape(d, NSA_KV_GROUPS * LANES)
    w_b = jnp.concatenate([w_in[:, split:main], w_in[:, main + ngate:], bg], axis=1)
    return w_in[:, :split].astype(BF16), w_b.astype(BF16)


def _nsa_mixer(h, g_pre, bn, s, w_in, pos_k, w1_k, w2_k, pos_v, w1_v, w2_v, rel_bias):
    g_n, dh = NSA_KV_GROUPS, NSA_HEAD_DIM
    w_a, w_b = _nsa_in_weights(w_in)
    proj_a = _norm_matmul(h, g_pre, w_a)
    proj_b = _norm_matmul(h, g_pre, w_b)
    nch = s // CMP_STRIDE
    p3 = proj_a.reshape(bn, nch, CMP_STRIDE, proj_a.shape[1])

    def chunks(c0):
        x = p3[..., c0:c0 + NSA_KV_WIDTH].reshape(bn, nch, CMP_STRIDE, g_n, dh)
        return jnp.transpose(x, (0, 3, 1, 2, 4)).reshape(bn * g_n * nch, CMP_STRIDE * dh)

    xkv = jnp.stack([chunks(NSA_WIDTH), chunks(NSA_WIDTH + NSA_KV_WIDTH)])
    half = CMP_STRIDE * dh
    pos = jnp.stack([pos_k.reshape(2, half), pos_v.reshape(2, half)])
    w1 = jnp.stack([w1_k.reshape(2, half, -1), w1_v.reshape(2, half, -1)]).astype(BF16)
    w2 = jnp.stack([w2_k, w2_v]).astype(BF16)
    cmp_kv, cmp_t = _compress(xkv, pos, w1, w2, nch)

    def value_tiles_t(c0):
        x = proj_b.reshape(bn, s // Q_BLOCK, Q_BLOCK, proj_b.shape[1])[..., c0:c0 + NSA_KV_WIDTH]
        x = x.reshape(bn, s // Q_BLOCK, Q_BLOCK, g_n, dh)
        return jnp.transpose(x, (0, 3, 1, 4, 2)).reshape(bn * g_n, s // Q_BLOCK, dh, Q_BLOCK)

    vs_t = value_tiles_t(0)
    vw_t = value_tiles_t(2 * NSA_KV_WIDTH)
    nd = min(s // Q_BLOCK, 15)
    bias = _causal_bias(rel_bias, nd)
    return _nsa_attention(proj_a, proj_b, cmp_kv, cmp_t, vs_t, vw_t, bias, bn, s)


def kernel(x, p, rel_bias, norm_pre, norm_post, ple_w_proj, ple_w_gate, a_w_in, a_conv_w, a_conv_b,
           a_w_r, a_b_r, a_w_i, a_b_i, a_lam, a_w_out, b_w_in, b_w_out, c_w_in, c_cmp_pos_k,
           c_cmp_w1_k, c_cmp_w2_k, c_cmp_pos_v, c_cmp_w1_v, c_cmp_w2_v, c_w_out):
    bn, s, d = x.shape
    depth = p.shape[0]
    t = bn * s
    h = x.reshape(t, d)
    for i in range(depth):
        kind = i % N_MIXERS
        j = i // N_MIXERS
        if kind == 0:
            proj = _norm_matmul(h, norm_pre[i], a_w_in[j].astype(BF16))
            y = _rglru(proj, bn, s, a_conv_w[j], a_conv_b[j], a_w_r[j], a_b_r[j], a_w_i[j],
                       a_b_i[j], a_lam[j])
            w_out = a_w_out[j]
        elif kind == 1:
            y = _dilated_mixer(h, norm_pre[i], bn, s, b_w_in[j], rel_bias)
            w_out = b_w_out[j]
        else:
            y = _nsa_mixer(h, norm_pre[i], bn, s, c_w_in[j], c_cmp_pos_k[j], c_cmp_w1_k[j],
                           c_cmp_w2_k[j], c_cmp_pos_v[j], c_cmp_w1_v[j], c_cmp_w2_v[j], rel_bias)
            w_out = c_w_out[j]
        h = _out_ple(y, w_out.astype(BF16), norm_post[i], h, p[i].reshape(t, -1),
                     ple_w_proj[i].astype(BF16), ple_w_gate[i].astype(BF16))
    return h.reshape(bn, s, d)
```

```python
import functools
import math

import numpy as np
import jax
import jax.numpy as jnp
from jax import lax
from jax.experimental import pallas as pl
from jax.experimental.pallas import tpu as pltpu

F32 = jnp.float32
BF16 = jnp.bfloat16

NORM_EPS = 1e-6
NEG_INF = -1e30
TINY = 1e-30
MASK_BIG = 2.0 ** 40

N_MIXERS = 3
N_BUCKETS = 32
BUCKET_MAX_DIST = 2048

LRU_BLOCKS = 8
CONV_W = 4
LRU_C = 8.0

DIL_PATTERNS = ((128, 1), (512, 4), (2048, 16))
DIL_HEADS = 16
DIL_HEAD_DIM = 64
DIL_WIDTH = DIL_HEADS * DIL_HEAD_DIM

NSA_HEADS = 16
NSA_KV_GROUPS = 4
NSA_REP = NSA_HEADS // NSA_KV_GROUPS
NSA_HEAD_DIM = 128
NSA_WIDTH = NSA_HEADS * NSA_HEAD_DIM
NSA_KV_WIDTH = NSA_KV_GROUPS * NSA_HEAD_DIM
CMP_BLOCK = 32
CMP_STRIDE = 16
SEL_BLOCK = 64
SEL_TOPK = 16
WIN = 512
FORCE_BONUS = 1e4

SUBLANES = 8
LANES = 128
Q_BLOCK = 128
KEY_TILE = 256
DIL_WINDOW = Q_BLOCK * max(d for _, d in DIL_PATTERNS)
SEL_TILE = 1024
NSA_QBLOCKS = 4
PROJ_ROWS = 512
PROJ_HALVES = 2
WIN_SPAN = WIN + Q_BLOCK
VMEM_LIMIT_BYTES = 56 * 1024 * 1024


def _params(*sem):
    return pltpu.CompilerParams(dimension_semantics=sem, vmem_limit_bytes=VMEM_LIMIT_BYTES)


def _dot(a, b):
    return jnp.dot(a, b, preferred_element_type=F32)


def _dot_nt(a, b):
    return lax.dot_general(a, b, (((1,), (1,)), ((), ())), preferred_element_type=F32)


def _sigmoid(x):
    return 1.0 / (1.0 + jnp.exp(-x))


def _silu(x):
    return x * _sigmoid(x)


def _resident(block_shape, index_map):
    return pl.BlockSpec(block_shape, index_map, pipeline_mode=pl.Buffered(1))


def _norm_matmul_kernel(h_ref, g_ref, w_ref, o_ref, *scratch, dil):
    n = w_ref.shape[1]
    half_rows = h_ref.shape[0] // PROJ_HALVES
    for hh in range(PROJ_HALVES):
        x = h_ref[hh * half_rows:(hh + 1) * half_rows, :]
        ms = jnp.mean(x * x, axis=-1, keepdims=True)
        u = (x * lax.rsqrt(ms + NORM_EPS) * g_ref[...]).astype(BF16)
        res = _dot(u, w_ref[...])
        if dil == 1:
            o_ref[hh * half_rows:(hh + 1) * half_rows, :] = res.astype(o_ref.dtype)
            continue
        res_scr = scratch[0]
        per = half_rows // dil
        for c in range(n // LANES):
            res_scr[hh, c] = res[:, c * LANES:(c + 1) * LANES]
        for r in range(dil):
            for c in range(n // LANES):
                o_ref[r, hh * per:(hh + 1) * per, c * LANES:(c + 1) * LANES] = (
                    res_scr[hh, c, pl.ds(r, per, stride=dil), :].astype(o_ref.dtype))


def _norm_matmul(h, g, w, dil=1, bn=1):
    t, d = h.shape
    n = w.shape[1]
    tm = min(t, PROJ_ROWS)
    in_specs = [
        pl.BlockSpec((tm, d), lambda i: (i, 0)),
        _resident((1, d), lambda i: (0, 0)),
        _resident((d, n), lambda i: (0, 0)),
    ]
    if dil > 1:
        s = t // bn
        nt = s // tm
        out_spec = pl.BlockSpec((None, dil, tm // dil, n), lambda i: (i // nt, 0, i % nt, 0))
        out_shape = jax.ShapeDtypeStruct((bn, dil, s // dil, n), BF16)
        scratch = [pltpu.VMEM((PROJ_HALVES, n // LANES, tm // PROJ_HALVES, LANES), F32)]
    else:
        out_spec = pl.BlockSpec((tm, n), lambda i: (i, 0))
        out_shape = jax.ShapeDtypeStruct((t, n), BF16)
        scratch = []
    return pl.pallas_call(
        functools.partial(_norm_matmul_kernel, dil=dil),
        grid=(t // tm,),
        in_specs=in_specs,
        out_specs=out_spec,
        out_shape=out_shape,
        scratch_shapes=scratch,
        compiler_params=_params("parallel"),
        name="norm_matmul" if dil == 1 else f"norm_matmul_residue_{dil}",
    )(h, g.reshape(1, d), w)


def _out_ple_kernel(y_ref, wo_ref, g_ref, h_ref, p_ref, wp_ref, wg_ref, o_ref):
    z = _dot(y_ref[...], wo_ref[...])
    ms = jnp.mean(z * z, axis=-1, keepdims=True)
    h1 = h_ref[...] + z * lax.rsqrt(ms + NORM_EPS) * g_ref[...]
    gate = _sigmoid(_dot(h1.astype(BF16), wg_ref[...]))
    pe = _dot(p_ref[...].astype(BF16), wp_ref[...])
    o_ref[...] = h1 + gate * pe


def _out_ple(y, w_out, g_post, h, p, w_proj, w_gate):
    t, d = h.shape
    k = y.shape[1]
    pd = p.shape[1]
    tm = min(t, 256)
    return pl.pallas_call(
        _out_ple_kernel,
        grid=(t // tm,),
        in_specs=[
            pl.BlockSpec((tm, k), lambda i: (i, 0)),
            _resident((k, d), lambda i: (0, 0)),
            _resident((1, d), lambda i: (0, 0)),
            pl.BlockSpec((tm, d), lambda i: (i, 0)),
            pl.BlockSpec((tm, pd), lambda i: (i, 0)),
            _resident((pd, d), lambda i: (0, 0)),
            _resident((d, d), lambda i: (0, 0)),
        ],
        out_specs=pl.BlockSpec((tm, d), lambda i: (i, 0)),
        out_shape=jax.ShapeDtypeStruct((t, d), F32),
        compiler_params=_params("parallel"),
        name="out_ple",
    )(y, w_out, g_post.reshape(1, d), h, p, w_proj, w_gate)


def _lru_kernel(xb_ref, gate_ref, cw_ref, cb_ref, wr_ref, br_ref, wi_ref, bi_ref, lam_ref,
                y_ref, xext_scr, h_scr, *, ts, bw):
    @pl.when(pl.program_id(1) == 0)
    def _():
        xext_scr[0:SUBLANES, :] = jnp.zeros((SUBLANES, xext_scr.shape[1]), F32)
        h_scr[...] = jnp.zeros(h_scr.shape, F32)

    row = lax.broadcasted_iota(jnp.int32, (SUBLANES, bw), 0)
    for n in range(LRU_BLOCKS):
        cols = slice(n * bw, (n + 1) * bw)
        x = xb_ref[:, cols].astype(F32)
        xext_scr[SUBLANES:SUBLANES + ts, cols] = x
        xc = cw_ref[3:4, cols] * x + cb_ref[:, cols]
        for k in range(1, CONV_W):
            xc = xc + cw_ref[3 - k:4 - k, cols] * xext_scr[SUBLANES - k:SUBLANES - k + ts, cols]
        xext_scr[0:SUBLANES, cols] = x[ts - SUBLANES:ts, :]

        xn = xc.astype(BF16)
        r = _sigmoid(_dot(xn, wr_ref[n]) + br_ref[:, cols])
        ig = _sigmoid(_dot(xn, wi_ref[n]) + bi_ref[:, cols])
        nl = -lam_ref[:, cols]
        softplus = jnp.maximum(nl, 0.0) + jnp.log(1.0 + jnp.exp(-jnp.abs(nl)))
        a = jnp.exp(-LRU_C * r * softplus)
        b = jnp.sqrt(1.0 - a * a) * (ig * xc)

        h = h_scr[:, cols]
        outs = []
        for grp in range(ts // SUBLANES):
            a8 = a[grp * SUBLANES:(grp + 1) * SUBLANES, :]
            b8 = b[grp * SUBLANES:(grp + 1) * SUBLANES, :]
            for dist in (1, 2, 4):
                a_sh = jnp.where(row >= dist, pltpu.roll(a8, dist, axis=0), 1.0)
                b_sh = jnp.where(row >= dist, pltpu.roll(b8, dist, axis=0), 0.0)
                b8 = a8 * b_sh + b8
                a8 = a8 * a_sh
            hh = a8 * h + b8
            h = jnp.broadcast_to(hh[SUBLANES - 1:SUBLANES, :], (SUBLANES, bw))
            outs.append(hh)
        h_scr[:, cols] = h
        hs = jnp.concatenate(outs, axis=0)
        y_ref[:, cols] = (hs * _silu(gate_ref[:, cols].astype(F32))).astype(BF16)


def _rglru(proj, bn, s, conv_w, conv_b, w_r, b_r, w_i, b_i, lam):
    t, two_c = proj.shape
    c = two_c // 2
    bw = c // LRU_BLOCKS
    ts = min(s, 256)
    nst = s // ts
    row = lambda v: v.reshape(1, c)
    kern = functools.partial(_lru_kernel, ts=ts, bw=bw)
    return pl.pallas_call(
        kern,
        grid=(bn, nst),
        in_specs=[
            pl.BlockSpec((ts, c), lambda b, i: (b * nst + i, 0)),
            pl.BlockSpec((ts, c), lambda b, i: (b * nst + i, 1)),
            pl.BlockSpec((CONV_W, c), lambda b, i: (0, 0)),
            pl.BlockSpec((1, c), lambda b, i: (0, 0)),
            pl.BlockSpec((LRU_BLOCKS, bw, bw), lambda b, i: (0, 0, 0)),
            pl.BlockSpec((1, c), lambda b, i: (0, 0)),
            pl.BlockSpec((LRU_BLOCKS, bw, bw), lambda b, i: (0, 0, 0)),
            pl.BlockSpec((1, c), lambda b, i: (0, 0)),
            pl.BlockSpec((1, c), lambda b, i: (0, 0)),
        ],
        out_specs=pl.BlockSpec((ts, c), lambda b, i: (b * nst + i, 0)),
        out_shape=jax.ShapeDtypeStruct((t, c), BF16),
        scratch_shapes=[pltpu.VMEM((ts + SUBLANES, c), F32), pltpu.VMEM((SUBLANES, c), F32)],
        compiler_params=_params("parallel", "arbitrary"),
        name="rglru",
    )(proj, proj, conv_w, row(conv_b), w_r.astype(BF16), row(b_r), w_i.astype(BF16), row(b_i),
      row(lam))


def _t5_bucket_np(dist):
    n = np.maximum(dist, 0)
    max_exact = N_BUCKETS // 2
    nf = np.maximum(n, max_exact).astype(np.float64)
    large = max_exact + (np.log(nf / max_exact) / math.log(BUCKET_MAX_DIST / max_exact)
                         * (N_BUCKETS - max_exact)).astype(np.int64)
    return np.where(n < max_exact, n, np.minimum(large, N_BUCKETS - 1)).astype(np.int32)


TOEPLITZ_PERIOD = Q_BLOCK + KEY_TILE


def _toeplitz(w, rows, cols):
    period = w.shape[-1]
    reps = -(-rows * (period - 1) // period)
    flat = jnp.tile(w, (1,) * (w.ndim - 1) + (reps,))[..., :rows * (period - 1)]
    return flat.reshape(w.shape[:-1] + (rows, period - 1))[..., :cols]


def _signed_offsets(n_cols):
    m = np.arange(TOEPLITZ_PERIOD)
    return np.where(m < n_cols, m, m - TOEPLITZ_PERIOD)


def _dilated_bias(rel_bias, max_delta, dilation):
    delta = Q_BLOCK - _signed_offsets(2 * Q_BLOCK)
    ok = (delta >= 0) & (delta <= max_delta)
    vec = jnp.where(jnp.asarray(ok)[:, None], rel_bias[_t5_bucket_np(delta * dilation)], NEG_INF)
    return _toeplitz(jnp.transpose(vec).astype(F32), Q_BLOCK, 2 * Q_BLOCK)


def _causal_bias(rel_bias, n_dist):
    heads = rel_bias.shape[1]
    dist = np.arange(n_dist)[:, None] * Q_BLOCK + _signed_offsets(Q_BLOCK)[None, :]
    vec = jnp.where(jnp.asarray(dist >= 0)[..., None], rel_bias[_t5_bucket_np(dist)], NEG_INF)
    vec = jnp.transpose(vec, (0, 2, 1)).astype(F32)
    vec = jnp.concatenate([vec, jnp.full((1, heads, TOEPLITZ_PERIOD), NEG_INF, F32)])
    tbl = _toeplitz(vec, KEY_TILE, Q_BLOCK)
    tbl = tbl.reshape(n_dist + 1, NSA_KV_GROUPS, NSA_REP, KEY_TILE, Q_BLOCK)
    tbl = jnp.transpose(tbl, (1, 0, 3, 2, 4))
    return tbl.reshape(NSA_KV_GROUPS, n_dist + 1, KEY_TILE, NSA_REP * Q_BLOCK)


def _dil_kernel(*refs, dils):
    ng = len(dils)
    groups = [refs[5 * g:5 * g + 5] for g in range(ng)]
    bias_refs = refs[5 * ng:6 * ng]
    gate_ref, y_ref = refs[6 * ng], refs[6 * ng + 1]
    ext = refs[6 * ng + 2:8 * ng + 2]
    o_acc, lse_acc = refs[8 * ng + 2:]
    first = pl.program_id(1) == 0
    scale = DIL_HEAD_DIM ** -0.5
    lane = lax.broadcasted_iota(jnp.int32, (Q_BLOCK, LANES), 1)
    low = lane < DIL_HEAD_DIM
    col = lax.broadcasted_iota(jnp.int32, (Q_BLOCK, 2 * Q_BLOCK), 1)
    halo_pen = jnp.where(col < Q_BLOCK, jnp.where(first, NEG_INF, 0.0), 0.0)
    for g, dil in enumerate(dils):
        q_ref, k_ref, kh_ref, v_ref, vh_ref = groups[g]
        kext, vext = ext[2 * g], ext[2 * g + 1]
        kext[:, 0:Q_BLOCK, :] = kh_ref[...]
        kext[:, Q_BLOCK:, :] = k_ref[...]
        vext[:, 0:Q_BLOCK, :] = vh_ref[...]
        vext[:, Q_BLOCK:, :] = v_ref[...]
        for r in range(dil):
            for j in range(k_ref.shape[1] // Q_BLOCK):
                qp = q_ref[r, j * Q_BLOCK:(j + 1) * Q_BLOCK, :]
                kb = kext[r, j * Q_BLOCK:(j + 2) * Q_BLOCK, :]
                vb = vext[r, j * Q_BLOCK:(j + 2) * Q_BLOCK, :]
                o_half, lse_half = [], []
                for half in range(2):
                    qm = jnp.where(low if half == 0 else jnp.logical_not(low), qp, jnp.zeros_like(qp))
                    sc = _dot_nt(qm, kb) * scale + bias_refs[g][half]
                    if j == 0:
                        sc = sc + halo_pen
                    m = jnp.max(sc, axis=-1, keepdims=True)
                    e = jnp.exp(sc - m)
                    l = jnp.sum(e, axis=-1, keepdims=True)
                    pv = _dot(e.astype(BF16), vb)
                    o_half.append(pv / l)
                    lse_half.append(jnp.broadcast_to(m + jnp.log(l), (Q_BLOCK, LANES)))
                o_new = jnp.where(low, o_half[0], o_half[1])
                lse_new = jnp.where(low, lse_half[0], lse_half[1])
                if dil == 1:
                    rows = pl.ds(j * Q_BLOCK, Q_BLOCK)
                else:
                    rows = pl.ds(r + dil * j * Q_BLOCK, Q_BLOCK, stride=dil)
                if g == 0:
                    o_acc[rows, :] = o_new
                    lse_acc[rows, :] = lse_new
                else:
                    o_old = o_acc[rows, :]
                    lse_old = lse_acc[rows, :]
                    top = jnp.maximum(lse_old, lse_new)
                    w_old = jnp.exp(lse_old - top)
                    w_new = jnp.exp(lse_new - top)
                    tot = w_old + w_new
                    o_acc[rows, :] = (w_old * o_old + w_new * o_new) / tot
                    if g + 1 < ng:
                        lse_acc[rows, :] = top + jnp.log(tot)
    y_ref[...] = (o_acc[...] * _silu(gate_ref[...].astype(F32))).astype(BF16)


def _dil_attention(arrays, biases, bn, s):
    dils = tuple(d for _, d in DIL_PATTERNS)
    win = DIL_WINDOW
    pairs = DIL_WIDTH // LANES
    nw = s // win
    in_specs, operands, scratch = [], [], []
    for g, dil in enumerate(dils):
        per = win // dil
        per_blocks = per // Q_BLOCK
        cur = lambda cb, dil=dil, per=per: pl.BlockSpec(
            (None, dil, per, LANES), lambda b, w, p: (b, 0, w, cb * pairs + p))
        halo = lambda cb, dil=dil, per_blocks=per_blocks: pl.BlockSpec(
            (None, dil, Q_BLOCK, LANES),
            lambda b, w, p: (b, 0, jnp.maximum(w * per_blocks - 1, 0), cb * pairs + p))
        in_specs += [cur(0), cur(1), halo(1), cur(2), halo(2)]
        operands += [arrays[g]] * 5
        scratch += [pltpu.VMEM((dil, per + Q_BLOCK, LANES), BF16)] * 2
    in_specs += [pl.BlockSpec((2, Q_BLOCK, 2 * Q_BLOCK), lambda b, w, p: (p, 0, 0))] * len(dils)
    operands += list(biases)
    in_specs.append(pl.BlockSpec((None, None, win, LANES), lambda b, w, p: (b, 0, w, 3 * pairs + p)))
    operands.append(arrays[0])
    scratch += [pltpu.VMEM((win, LANES), F32)] * 2
    return pl.pallas_call(
        functools.partial(_dil_kernel, dils=dils),
        grid=(bn, nw, pairs),
        in_specs=in_specs,
        out_specs=pl.BlockSpec((win, LANES), lambda b, w, p: (b * nw + w, p)),
        out_shape=jax.ShapeDtypeStruct((bn * s, DIL_WIDTH), BF16),
        scratch_shapes=scratch,
        compiler_params=_params("parallel", "arbitrary", "arbitrary"),
        name="dilated_attention",
    )(*operands)


def _dilated_mixer(h, g_pre, bn, s, w_in, rel_bias):
    w = DIL_WIDTH
    ng = len(DIL_PATTERNS)
    wb = w_in.astype(BF16)
    main_w = jnp.concatenate([wb[:, :3 * w], wb[:, 3 * ng * w:]], axis=1)
    arrays = [_norm_matmul(h, g_pre, main_w).reshape(bn, 1, s, 4 * w)]
    for gi in range(1, ng):
        dil = DIL_PATTERNS[gi][1]
        arrays.append(_norm_matmul(h, g_pre, wb[:, 3 * gi * w:3 * (gi + 1) * w], dil=dil, bn=bn))
    biases = [_dilated_bias(rel_bias, window // dil, dil) for window, dil in DIL_PATTERNS]
    return _dil_attention(arrays, biases, bn, s)


def _gelu_tanh(x):
    return 0.5 * x * (1.0 + jnp.tanh(math.sqrt(2.0 / math.pi) * (x + 0.044715 * (x * x * x))))


def _compress_kernel(x_ref, pos_ref, w1_ref, w2_ref, o_ref, ot_ref, *, nch):
    x = x_ref[...].astype(F32)
    top = _dot((x + pos_ref[0:1, :]).astype(BF16), w1_ref[0])
    bot = _dot((x + pos_ref[1:2, :]).astype(BF16), w1_ref[1])
    hid = top + pltpu.roll(bot, nch - 1, axis=0)
    out = _dot(_gelu_tanh(hid).astype(BF16), w2_ref[...])
    o_ref[...] = out.astype(BF16)
    ot_ref[...] = jnp.transpose(out).astype(BF16)


def _compress(xkv, pos, w1, w2, nch):
    _, rows, width = xkv.shape
    hidden = w1.shape[-1]
    dh = w2.shape[-1]
    return pl.pallas_call(
        functools.partial(_compress_kernel, nch=nch),
        grid=(2, rows // nch),
        in_specs=[
            pl.BlockSpec((None, nch, width), lambda kv, i: (kv, i, 0)),
            pl.BlockSpec((None, 2, width), lambda kv, i: (kv, 0, 0)),
            pl.BlockSpec((None, 2, width, hidden), lambda kv, i: (kv, 0, 0, 0)),
            pl.BlockSpec((None, hidden, dh), lambda kv, i: (kv, 0, 0)),
        ],
        out_specs=[pl.BlockSpec((None, nch, dh), lambda kv, i: (kv, i, 0)),
                   pl.BlockSpec((None, None, dh, nch), lambda kv, i: (kv, i, 0, 0))],
        out_shape=[jax.ShapeDtypeStruct((2, rows, dh), BF16),
                   jax.ShapeDtypeStruct((2, rows // nch, dh, nch), BF16)],
        compiler_params=_params("parallel", "parallel"),
        name="nsa_compress",
    )(xkv, pos, w1, w2)


def _bias_tile(bias_ref, d, nd):
    return bias_ref[jnp.where(d < 0, nd, jnp.minimum(d, nd - 1))]


def _nsa_front(i, q_blk, kc_ref, vct_ref, kw_ref, vwt_ref, bias_ref, cov_ref, *, nsel, topk, nd):
    t0 = i * Q_BLOCK
    dh = NSA_HEAD_DIM
    rq = NSA_REP * Q_BLOCK
    scale = dh ** -0.5
    q_all = jnp.concatenate([q_blk[:, r * dh:(r + 1) * dh] for r in range(NSA_REP)], axis=0)
    qs = (q_all.astype(F32) * scale).astype(BF16)

    def bias_tile(d):
        return _bias_tile(bias_ref, d, nd)

    def per_head(x):
        return jnp.concatenate([x] * NSA_REP, axis=1)

    kc = kc_ref[...]
    nch = kc.shape[0]
    t_q = t0 + (lax.broadcasted_iota(jnp.int32, (nch, rq), 1) & (Q_BLOCK - 1))
    cend = lax.broadcasted_iota(jnp.int32, (nch, rq), 0) * CMP_STRIDE + (CMP_BLOCK - 1)
    cmask = cend <= t_q
    sc = jnp.where(cmask, _dot_nt(kc, qs), NEG_INF)
    m = jnp.max(sc, axis=0, keepdims=True)
    e = jnp.where(cmask, jnp.exp(sc - m), 0.0)
    prob = e / jnp.maximum(jnp.sum(e, axis=0, keepdims=True), TINY)
    o_cmp = _dot(vct_ref[...], prob.astype(BF16))
    psum = prob[:, 0:Q_BLOCK]
    for r in range(1, NSA_REP):
        psum = psum + prob[:, r * Q_BLOCK:(r + 1) * Q_BLOCK]
    p_hi = psum.astype(BF16)
    p_lo = (psum - p_hi.astype(F32)).astype(BF16)
    cov = cov_ref[...]
    imp = (_dot(cov, p_hi) + _dot(cov, p_lo))[0:nsel, :]
    blk = lax.broadcasted_iota(jnp.int32, (nsel, Q_BLOCK), 0)
    t_col = t0 + lax.broadcasted_iota(jnp.int32, (nsel, Q_BLOCK), 1)
    cur = t_col // SEL_BLOCK
    forced = (blk == 0) | (blk == cur) | (blk == cur - 1)
    valid = blk * SEL_BLOCK <= t_col
    imp = jnp.where(valid, imp + jnp.where(forced, FORCE_BONUS, 0.0), NEG_INF)
    ngrp = nsel // SUBLANES
    rank = [jnp.zeros((SUBLANES, Q_BLOCK), jnp.int32) for _ in range(ngrp)]
    parts = [imp[g * SUBLANES:(g + 1) * SUBLANES, :] for g in range(ngrp)]
    sub = lax.broadcasted_iota(jnp.int32, (SUBLANES, Q_BLOCK), 0)
    for mrow in range(nsel):
        other = jnp.broadcast_to(imp[mrow:mrow + 1, :], (SUBLANES, Q_BLOCK))
        for g in range(ngrp):
            ge = jnp.where(other >= parts[g], 1, 0)
            gt = jnp.where(other > parts[g], 1, 0)
            if g * SUBLANES > mrow:
                before = ge
            elif (g + 1) * SUBLANES - 1 < mrow:
                before = gt
            else:
                before = jnp.where(sub + g * SUBLANES > mrow, ge, gt)
            rank[g] = rank[g] + before
    rank = jnp.concatenate(rank, axis=0)
    sel_t = jnp.where(valid & (rank < topk), 1.0, 0.0)
    if nsel < LANES:
        sel_t = jnp.concatenate([sel_t, jnp.zeros((LANES - nsel, Q_BLOCK), F32)], axis=0)

    j0 = jnp.maximum(i - WIN // Q_BLOCK, 0)
    start = pl.multiple_of(j0 * Q_BLOCK, Q_BLOCK)
    dbase = i - j0
    sc = _dot_nt(kw_ref[pl.ds(start, WIN_SPAN), :], qs)
    dist = (dbase * Q_BLOCK + lax.broadcasted_iota(jnp.int32, (Q_BLOCK, Q_BLOCK), 1)
            - lax.broadcasted_iota(jnp.int32, (Q_BLOCK, Q_BLOCK), 0))
    too_old = per_head(jnp.where(dist <= WIN - 1, 0.0, NEG_INF))
    n_full, rem = divmod(WIN_SPAN, KEY_TILE)
    pieces = [bias_tile(dbase - (KEY_TILE // Q_BLOCK) * u) for u in range(n_full)]
    if rem:
        pieces.append(bias_tile(dbase - (KEY_TILE // Q_BLOCK) * n_full)[0:rem])
    pieces[0] = jnp.concatenate([pieces[0][0:Q_BLOCK] + too_old, pieces[0][Q_BLOCK:]], axis=0)
    sc = sc + jnp.concatenate(pieces, axis=0)
    m = jnp.max(sc, axis=0, keepdims=True)
    e = jnp.exp(sc - m)
    l_win = jnp.sum(e, axis=0, keepdims=True)
    vt = jnp.concatenate([vwt_ref[j0 + u] for u in range(WIN_SPAN // Q_BLOCK)], axis=1)
    o_win = _dot(vt, e.astype(BF16)) / jnp.maximum(l_win, TINY)

    sel_pen = jnp.transpose(jnp.where(sel_t > 0.5, 0.0, -MASK_BIG)).astype(BF16)
    return qs, o_cmp, o_win, sel_pen


def _nsa_kernel(q_ref, kc_ref, vct_ref, ks_ref, vst_ref, kw_ref, vwt_ref, gp_ref, bg_ref,
                bias_ref, cov_ref, exp_ref, y_ref, acc_scr, *, nsel, topk, nd):
    step = pl.program_id(2)
    rq = NSA_REP * Q_BLOCK
    cols = NSA_QBLOCKS * rq
    blocks = [step * NSA_QBLOCKS + qb for qb in range(NSA_QBLOCKS)]
    fronts = [
        _nsa_front(i, q_ref[qb * Q_BLOCK:(qb + 1) * Q_BLOCK, :], kc_ref, vct_ref, kw_ref, vwt_ref,
                   bias_ref, cov_ref, nsel=nsel, topk=topk, nd=nd)
        for qb, i in enumerate(blocks)]

    sub_tiles = SEL_TILE // Q_BLOCK
    q_aug = jnp.concatenate(
        [jnp.concatenate([qs, jnp.concatenate([sel_pen] * NSA_REP, axis=0)], axis=1)
         for qs, _, _, sel_pen in fronts], axis=0)
    acc_scr[...] = jnp.zeros(acc_scr.shape, F32)

    def sel_body(kt, carry):
        m_prev, l_prev = carry
        k0 = pl.multiple_of(kt * SEL_TILE, SEL_TILE)
        k_aug = jnp.concatenate([ks_ref[pl.ds(k0, SEL_TILE), :], exp_ref[kt]], axis=1)
        bias = jnp.concatenate(
            [jnp.concatenate(
                [_bias_tile(bias_ref, i - sub_tiles * kt - (KEY_TILE // Q_BLOCK) * u, nd) for i in blocks],
                axis=1)
             for u in range(SEL_TILE // KEY_TILE)], axis=0)
        sc = _dot_nt(k_aug, q_aug) + bias
        m_new = jnp.maximum(m_prev, jnp.max(sc, axis=0, keepdims=True))
        alpha = jnp.exp(m_prev - m_new)
        e = jnp.exp(sc - m_new)
        l_new = alpha * l_prev + jnp.sum(e, axis=0, keepdims=True)
        vt = jnp.concatenate([vst_ref[kt * sub_tiles + u] for u in range(sub_tiles)], axis=1)
        acc_scr[...] = alpha * acc_scr[...] + _dot(vt, e.astype(BF16))
        return m_new, l_new

    init = (jnp.full((1, cols), NEG_INF, F32), jnp.zeros((1, cols), F32))
    _, l_sel = lax.fori_loop(0, blocks[-1] // sub_tiles + 1, sel_body, init)
    o_sel = acc_scr[...] / jnp.maximum(l_sel, TINY)
    for qb, (_, o_cmp, o_win, _) in enumerate(fronts):
        rows = slice(qb * Q_BLOCK, (qb + 1) * Q_BLOCK)
        _nsa_back(o_cmp, o_sel[:, qb * rq:(qb + 1) * rq], o_win, bg_ref[rows, :], gp_ref, y_ref, rows)


def _nsa_back(o_cmp, o_sel, o_win, branch_gates, gp_ref, y_ref, rows):
    dh = NSA_HEAD_DIM
    gates_t = jnp.transpose(_sigmoid(branch_gates.astype(F32)))

    def gate_row(branch):
        return jnp.concatenate(
            [gates_t[3 * r + branch:3 * r + branch + 1, :] for r in range(NSA_REP)], axis=1)

    o_t = gate_row(0) * o_cmp + gate_row(1) * o_sel + gate_row(2) * o_win
    for r in range(NSA_REP):
        o = jnp.transpose(o_t[:, r * Q_BLOCK:(r + 1) * Q_BLOCK])
        gp = gp_ref[rows, r * dh:(r + 1) * dh].astype(F32)
        y_ref[rows, r * dh:(r + 1) * dh] = (o * _silu(gp)).astype(BF16)


def _nsa_layout(s):
    nch = s // CMP_STRIDE
    nc = (s - CMP_BLOCK) // CMP_STRIDE + 1
    nsel = s // SEL_BLOCK
    start = np.arange(nc) * CMP_STRIDE
    sel_start = np.arange(nsel) * SEL_BLOCK
    cover = (start[:, None] < sel_start[None, :] + SEL_BLOCK) & (
        start[:, None] + CMP_BLOCK - 1 >= sel_start[None, :])
    cov_t = np.zeros((LANES, nch), np.float32)
    cov_t[:nsel, :nc] = cover.T
    key_blk = np.arange(s) // SEL_BLOCK
    expand = (key_blk[:, None] == np.arange(LANES)[None, :]).astype(np.float32)
    expand = expand.reshape(s // SEL_TILE, SEL_TILE, LANES)
    return jnp.asarray(cov_t, BF16), jnp.asarray(expand, BF16)


def _nsa_attention(proj_a, proj_b, cmp_kv, cmp_t, vs_t, vw_t, bias, bn, s):
    t = proj_a.shape[0]
    g_n, dh, rep = NSA_KV_GROUPS, NSA_HEAD_DIM, NSA_REP
    gw = rep * dh
    nq = s // Q_BLOCK
    nch = s // CMP_STRIDE
    nsel = s // SEL_BLOCK
    nd = bias.shape[1] - 1
    cov_t, expand = _nsa_layout(s)
    ksel0 = (NSA_WIDTH + 2 * NSA_KV_WIDTH) // dh
    kwin0 = NSA_KV_WIDTH // dh
    kcol = lambda c0: pl.BlockSpec((s, dh), lambda b, g, i: (b, c0 + g))
    vtile = pl.BlockSpec((None, nq, dh, Q_BLOCK), lambda b, g, i: (b * g_n + g, 0, 0, 0))
    gp0 = 3 * NSA_KV_WIDTH // gw
    bg0 = (3 * NSA_KV_WIDTH + NSA_WIDTH) // LANES
    kern = functools.partial(_nsa_kernel, nsel=nsel, topk=min(SEL_TOPK, nsel), nd=nd)
    steps = nq // NSA_QBLOCKS
    rows = NSA_QBLOCKS * Q_BLOCK
    return pl.pallas_call(
        kern,
        grid=(bn, g_n, steps),
        in_specs=[
            pl.BlockSpec((rows, gw), lambda b, g, i: (b * steps + i, g)),
            pl.BlockSpec((None, nch, dh), lambda b, g, i: (0, b * g_n + g, 0)),
            pl.BlockSpec((None, None, dh, nch), lambda b, g, i: (1, b * g_n + g, 0, 0)),
            kcol(ksel0), vtile, kcol(kwin0), vtile,
            pl.BlockSpec((rows, gw), lambda b, g, i: (b * steps + i, gp0 + g)),
            pl.BlockSpec((rows, LANES), lambda b, g, i: (b * steps + i, bg0 + g)),
            pl.BlockSpec((None, nd + 1, KEY_TILE, rep * Q_BLOCK), lambda b, g, i: (g, 0, 0, 0),
                         pipeline_mode=pl.Buffered(1)),
            _resident(cov_t.shape, lambda b, g, i: (0, 0)),
            _resident(expand.shape, lambda b, g, i: (0, 0, 0)),
        ],
        out_specs=pl.BlockSpec((rows, gw), lambda b, g, i: (b * steps + i, g)),
        out_shape=jax.ShapeDtypeStruct((t, NSA_WIDTH), BF16),
        scratch_shapes=[pltpu.VMEM((dh, NSA_QBLOCKS * rep * Q_BLOCK), F32)],
        compiler_params=_params("parallel", "parallel", "arbitrary"),
        name="nsa_attention",
    )(proj_a, cmp_kv, cmp_t, proj_a, vs_t, proj_b, vw_t, proj_b, proj_b, bias, cov_t, expand)


def _nsa_in_weights(w_in):
    split = NSA_WIDTH + 3 * NSA_KV_WIDTH
    main = NSA_WIDTH + 6 * NSA_KV_WIDTH
    ngate = 3 * NSA_HEADS
    per_group = ngate // NSA_KV_GROUPS
    d = w_in.shape[0]
    bg = w_in[:, main:main + ngate].reshape(d, NSA_KV_GROUPS, per_group)
    bg = jnp.pad(bg, ((0, 0), (0, 0), (0, LANES - per_group))).reshape(d, NSA_KV_GROUPS * LANES)
    w_b = jnp.concatenate([w_in[:, split:main], w_in[:, main + ngate:], bg], axis=1)
    return w_in[:, :split].astype(BF16), w_b.astype(BF16)


def _nsa_mixer(h, g_pre, bn, s, w_in, pos_k, w1_k, w2_k, pos_v, w1_v, w2_v, rel_bias):
    g_n, dh = NSA_KV_GROUPS, NSA_HEAD_DIM
    w_a, w_b = _nsa_in_weights(w_in)
    proj_a = _norm_matmul(h, g_pre, w_a)
    proj_b = _norm_matmul(h, g_pre, w_b)
    nch = s // CMP_STRIDE
    p3 = proj_a.reshape(bn, nch, CMP_STRIDE, proj_a.shape[1])

    def chunks(c0):
        x = p3[..., c0:c0 + NSA_KV_WIDTH].reshape(bn, nch, CMP_STRIDE, g_n, dh)
        return jnp.transpose(x, (0, 3, 1, 2, 4)).reshape(bn * g_n * nch, CMP_STRIDE * dh)

    xkv = jnp.stack([chunks(NSA_WIDTH), chunks(NSA_WIDTH + NSA_KV_WIDTH)])
    half = CMP_STRIDE * dh
    pos = jnp.stack([pos_k.reshape(2, half), pos_v.reshape(2, half)])
    w1 = jnp.stack([w1_k.reshape(2, half, -1), w1_v.reshape(2, half, -1)]).astype(BF16)
    w2 = jnp.stack([w2_k, w2_v]).astype(BF16)
    cmp_kv, cmp_t = _compress(xkv, pos, w1, w2, nch)

    def value_tiles_t(c0):
        x = proj_b.reshape(bn, s // Q_BLOCK, Q_BLOCK, proj_b.shape[1])[..., c0:c0 + NSA_KV_WIDTH]
        x = x.reshape(bn, s // Q_BLOCK, Q_BLOCK, g_n, dh)
        return jnp.transpose(x, (0, 3, 1, 4, 2)).reshape(bn * g_n, s // Q_BLOCK, dh, Q_BLOCK)

    vs_t = value_tiles_t(0)
    vw_t = value_tiles_t(2 * NSA_KV_WIDTH)
    nd = min(s // Q_BLOCK, 15)
    bias = _causal_bias(rel_bias, nd)
    return _nsa_attention(proj_a, proj_b, cmp_kv, cmp_t, vs_t, vw_t, bias, bn, s)


def kernel(x, p, rel_bias, norm_pre, norm_post, ple_w_proj, ple_w_gate, a_w_in, a_conv_w, a_conv_b,
           a_w_r, a_b_r, a_w_i, a_b_i, a_lam, a_w_out, b_w_in, b_w_out, c_w_in, c_cmp_pos_k,
           c_cmp_w1_k, c_cmp_w2_k, c_cmp_pos_v, c_cmp_w1_v, c_cmp_w2_v, c_w_out):
    bn, s, d = x.shape
    depth = p.shape[0]
    t = bn * s
    h = x.reshape(t, d)
    for i in range(depth):
        kind = i % N_MIXERS
        j = i // N_MIXERS
        if kind == 0:
            proj = _norm_matmul(h, norm_pre[i], a_w_in[j].astype(BF16))
            y = _rglru(proj, bn, s, a_conv_w[j], a_conv_b[j], a_w_r[j], a_b_r[j], a_w_i[j],
                       a_b_i[j], a_lam[j])
            w_out = a_w_out[j]
        elif kind == 1:
            y = _dilated_mixer(h, norm_pre[i], bn, s, b_w_in[j], rel_bias)
            w_out = b_w_out[j]
        else:
            y = _nsa_mixer(h, norm_pre[i], bn, s, c_w_in[j], c_cmp_pos_k[j], c_cmp_w1_k[j],
                           c_cmp_w2_k[j], c_cmp_pos_v[j], c_cmp_w1_v[j], c_cmp_w2_v[j], rel_bias)
            w_out = c_w_out[j]
        h = _out_ple(y, w_out.astype(BF16), norm_post[i], h, p[i].reshape(t, -1),
                     ple_w_proj[i].astype(BF16), ple_w_gate[i].astype(BF16))
    return h.reshape(bn, s, d)
```

```python
import functools
import math

import numpy as np
import jax
import jax.numpy as jnp
from jax import lax
from jax.experimental import pallas as pl
from jax.experimental.pallas import tpu as pltpu

F32 = jnp.float32
BF16 = jnp.bfloat16

NORM_EPS = 1e-6
NEG_INF = -1e30
TINY = 1e-30
MASK_BIG = 2.0 ** 40
LOG2_E = math.log2(math.e)

N_MIXERS = 3
N_BUCKETS = 32
BUCKET_MAX_DIST = 2048

LRU_BLOCKS = 8
CONV_W = 4
LRU_C = 8.0

DIL_PATTERNS = ((128, 1), (512, 4), (2048, 16))
DIL_HEADS = 16
DIL_HEAD_DIM = 64
DIL_WIDTH = DIL_HEADS * DIL_HEAD_DIM

NSA_HEADS = 16
NSA_KV_GROUPS = 4
NSA_REP = NSA_HEADS // NSA_KV_GROUPS
NSA_HEAD_DIM = 128
NSA_WIDTH = NSA_HEADS * NSA_HEAD_DIM
NSA_KV_WIDTH = NSA_KV_GROUPS * NSA_HEAD_DIM
CMP_BLOCK = 32
CMP_STRIDE = 16
SEL_BLOCK = 64
SEL_TOPK = 16
WIN = 512
FORCE_BONUS = 1e4

SUBLANES = 8
LANES = 128
Q_BLOCK = 128
KEY_TILE = 256
DIL_WINDOW = Q_BLOCK * max(d for _, d in DIL_PATTERNS)
SEL_TILE = 1024
NSA_QBLOCKS = 4
PROJ_ROWS = 512
PROJ_HALVES = 2
WIN_SPAN = WIN + Q_BLOCK
VMEM_LIMIT_BYTES = 56 * 1024 * 1024


def _params(*sem):
    return pltpu.CompilerParams(dimension_semantics=sem, vmem_limit_bytes=VMEM_LIMIT_BYTES)


def _dot(a, b):
    return jnp.dot(a, b, preferred_element_type=F32)


def _dot_nt(a, b):
    return lax.dot_general(a, b, (((1,), (1,)), ((), ())), preferred_element_type=F32)


def _dot_tn(a, b):
    return lax.dot_general(a, b, (((0,), (0,)), ((), ())), preferred_element_type=F32)


def _sigmoid(x):
    return 1.0 / (1.0 + jnp.exp(-x))


def _silu(x):
    return x * _sigmoid(x)


def _resident(block_shape, index_map):
    return pl.BlockSpec(block_shape, index_map, pipeline_mode=pl.Buffered(1))


def _norm_matmul_kernel(h_ref, g_ref, w_ref, o_ref, *scratch, dil):
    n = w_ref.shape[1]
    half_rows = h_ref.shape[0] // PROJ_HALVES
    for hh in range(PROJ_HALVES):
        x = h_ref[hh * half_rows:(hh + 1) * half_rows, :]
        ms = jnp.mean(x * x, axis=-1, keepdims=True)
        u = (x * lax.rsqrt(ms + NORM_EPS) * g_ref[...]).astype(BF16)
        res = _dot(u, w_ref[...])
        if dil == 1:
            o_ref[hh * half_rows:(hh + 1) * half_rows, :] = res.astype(o_ref.dtype)
            continue
        res_scr = scratch[0]
        per = half_rows // dil
        for c in range(n // LANES):
            res_scr[hh, c] = res[:, c * LANES:(c + 1) * LANES]
        for r in range(dil):
            for c in range(n // LANES):
                o_ref[r, hh * per:(hh + 1) * per, c * LANES:(c + 1) * LANES] = (
                    res_scr[hh, c, pl.ds(r, per, stride=dil), :].astype(o_ref.dtype))


def _norm_matmul(h, g, w, dil=1, bn=1):
    t, d = h.shape
    n = w.shape[1]
    tm = min(t, PROJ_ROWS)
    in_specs = [
        pl.BlockSpec((tm, d), lambda i: (i, 0)),
        _resident((1, d), lambda i: (0, 0)),
        _resident((d, n), lambda i: (0, 0)),
    ]
    if dil > 1:
        s = t // bn
        nt = s // tm
        out_spec = pl.BlockSpec((None, dil, tm // dil, n), lambda i: (i // nt, 0, i % nt, 0))
        out_shape = jax.ShapeDtypeStruct((bn, dil, s // dil, n), BF16)
        scratch = [pltpu.VMEM((PROJ_HALVES, n // LANES, tm // PROJ_HALVES, LANES), F32)]
    else:
        out_spec = pl.BlockSpec((tm, n), lambda i: (i, 0))
        out_shape = jax.ShapeDtypeStruct((t, n), BF16)
        scratch = []
    return pl.pallas_call(
        functools.partial(_norm_matmul_kernel, dil=dil),
        grid=(t // tm,),
        in_specs=in_specs,
        out_specs=out_spec,
        out_shape=out_shape,
        scratch_shapes=scratch,
        compiler_params=_params("parallel"),
        name="norm_matmul" if dil == 1 else f"norm_matmul_residue_{dil}",
    )(h, g.reshape(1, d), w)


def _out_ple_kernel(y_ref, wo_ref, g_ref, h_ref, p_ref, wp_ref, wg_ref, o_ref):
    z = _dot(y_ref[...], wo_ref[...])
    ms = jnp.mean(z * z, axis=-1, keepdims=True)
    h1 = h_ref[...] + z * lax.rsqrt(ms + NORM_EPS) * g_ref[...]
    gate = _sigmoid(_dot(h1.astype(BF16), wg_ref[...]))
    pe = _dot(p_ref[...].astype(BF16), wp_ref[...])
    o_ref[...] = h1 + gate * pe


def _out_ple(y, w_out, g_post, h, p, w_proj, w_gate):
    t, d = h.shape
    k = y.shape[1]
    pd = p.shape[1]
    tm = min(t, 256)
    return pl.pallas_call(
        _out_ple_kernel,
        grid=(t // tm,),
        in_specs=[
            pl.BlockSpec((tm, k), lambda i: (i, 0)),
            _resident((k, d), lambda i: (0, 0)),
            _resident((1, d), lambda i: (0, 0)),
            pl.BlockSpec((tm, d), lambda i: (i, 0)),
            pl.BlockSpec((tm, pd), lambda i: (i, 0)),
            _resident((pd, d), lambda i: (0, 0)),
            _resident((d, d), lambda i: (0, 0)),
        ],
        out_specs=pl.BlockSpec((tm, d), lambda i: (i, 0)),
        out_shape=jax.ShapeDtypeStruct((t, d), F32),
        compiler_params=_params("parallel"),
        name="out_ple",
    )(y, w_out, g_post.reshape(1, d), h, p, w_proj, w_gate)


def _lru_kernel(xb_ref, gate_ref, cw_ref, cb_ref, wr_ref, br_ref, wi_ref, bi_ref, lam_ref,
                y_ref, xext_scr, h_scr, *, ts, bw):
    @pl.when(pl.program_id(1) == 0)
    def _():
        xext_scr[0:SUBLANES, :] = jnp.zeros((SUBLANES, xext_scr.shape[1]), F32)
        h_scr[...] = jnp.zeros(h_scr.shape, F32)

    row = lax.broadcasted_iota(jnp.int32, (SUBLANES, bw), 0)
    for n in range(LRU_BLOCKS):
        cols = slice(n * bw, (n + 1) * bw)
        x = xb_ref[:, cols].astype(F32)
        xext_scr[SUBLANES:SUBLANES + ts, cols] = x
        xc = cw_ref[3:4, cols] * x + cb_ref[:, cols]
        for k in range(1, CONV_W):
            xc = xc + cw_ref[3 - k:4 - k, cols] * xext_scr[SUBLANES - k:SUBLANES - k + ts, cols]
        xext_scr[0:SUBLANES, cols] = x[ts - SUBLANES:ts, :]

        xn = xc.astype(BF16)
        r = _sigmoid(_dot(xn, wr_ref[n]) + br_ref[:, cols])
        ig = _sigmoid(_dot(xn, wi_ref[n]) + bi_ref[:, cols])
        nl = -lam_ref[:, cols]
        softplus = jnp.maximum(nl, 0.0) + jnp.log(1.0 + jnp.exp(-jnp.abs(nl)))
        a = jnp.exp(-LRU_C * r * softplus)
        b = jnp.sqrt(1.0 - a * a) * (ig * xc)

        h = h_scr[:, cols]
        outs = []
        for grp in range(ts // SUBLANES):
            a8 = a[grp * SUBLANES:(grp + 1) * SUBLANES, :]
            b8 = b[grp * SUBLANES:(grp + 1) * SUBLANES, :]
            for dist in (1, 2, 4):
                a_sh = jnp.where(row >= dist, pltpu.roll(a8, dist, axis=0), 1.0)
                b_sh = jnp.where(row >= dist, pltpu.roll(b8, dist, axis=0), 0.0)
                b8 = a8 * b_sh + b8
                a8 = a8 * a_sh
            hh = a8 * h + b8
            h = jnp.broadcast_to(hh[SUBLANES - 1:SUBLANES, :], (SUBLANES, bw))
            outs.append(hh)
        h_scr[:, cols] = h
        hs = jnp.concatenate(outs, axis=0)
        y_ref[:, cols] = (hs * _silu(gate_ref[:, cols].astype(F32))).astype(BF16)


def _rglru(proj, bn, s, conv_w, conv_b, w_r, b_r, w_i, b_i, lam):
    t, two_c = proj.shape
    c = two_c // 2
    bw = c // LRU_BLOCKS
    ts = min(s, 256)
    nst = s // ts
    row = lambda v: v.reshape(1, c)
    kern = functools.partial(_lru_kernel, ts=ts, bw=bw)
    return pl.pallas_call(
        kern,
        grid=(bn, nst),
        in_specs=[
            pl.BlockSpec((ts, c), lambda b, i: (b * nst + i, 0)),
            pl.BlockSpec((ts, c), lambda b, i: (b * nst + i, 1)),
            pl.BlockSpec((CONV_W, c), lambda b, i: (0, 0)),
            pl.BlockSpec((1, c), lambda b, i: (0, 0)),
            pl.BlockSpec((LRU_BLOCKS, bw, bw), lambda b, i: (0, 0, 0)),
            pl.BlockSpec((1, c), lambda b, i: (0, 0)),
            pl.BlockSpec((LRU_BLOCKS, bw, bw), lambda b, i: (0, 0, 0)),
            pl.BlockSpec((1, c), lambda b, i: (0, 0)),
            pl.BlockSpec((1, c), lambda b, i: (0, 0)),
        ],
        out_specs=pl.BlockSpec((ts, c), lambda b, i: (b * nst + i, 0)),
        out_shape=jax.ShapeDtypeStruct((t, c), BF16),
        scratch_shapes=[pltpu.VMEM((ts + SUBLANES, c), F32), pltpu.VMEM((SUBLANES, c), F32)],
        compiler_params=_params("parallel", "arbitrary"),
        name="rglru",
    )(proj, proj, conv_w, row(conv_b), w_r.astype(BF16), row(b_r), w_i.astype(BF16), row(b_i),
      row(lam))


def _t5_bucket_np(dist):
    n = np.maximum(dist, 0)
    max_exact = N_BUCKETS // 2
    nf = np.maximum(n, max_exact).astype(np.float64)
    large = max_exact + (np.log(nf / max_exact) / math.log(BUCKET_MAX_DIST / max_exact)
                         * (N_BUCKETS - max_exact)).astype(np.int64)
    return np.where(n < max_exact, n, np.minimum(large, N_BUCKETS - 1)).astype(np.int32)


TOEPLITZ_PERIOD = Q_BLOCK + KEY_TILE


def _toeplitz_kernel(w_ref, o_ref, *, rows, cols, side_by_side, scale):
    for h in range(w_ref.shape[0]):
        gen = jnp.broadcast_to(w_ref[h:h + 1, :] * scale, (rows, w_ref.shape[1]))
        tile = pltpu.roll(gen, 0, 1, stride=1, stride_axis=0)[:, :cols]
        o_ref[h // side_by_side, :, (h % side_by_side) * cols:(h % side_by_side + 1) * cols] = tile


def _toeplitz(w, rows, cols, side_by_side=1, scale=1.0):
    n, heads, period = w.shape
    out_block = (None, heads // side_by_side, rows, side_by_side * cols)
    return pl.pallas_call(
        functools.partial(_toeplitz_kernel, rows=rows, cols=cols, side_by_side=side_by_side, scale=scale),
        grid=(n,),
        in_specs=[pl.BlockSpec((None, heads, period), lambda i: (i, 0, 0))],
        out_specs=pl.BlockSpec(out_block, lambda i: (i, 0, 0, 0)),
        out_shape=jax.ShapeDtypeStruct((n,) + out_block[1:], F32),
        compiler_params=_params("parallel"),
        name="toeplitz_bias",
    )(w)


def _signed_offsets(n_cols):
    m = np.arange(TOEPLITZ_PERIOD)
    return np.where(m < n_cols, m, m - TOEPLITZ_PERIOD)


def _dilated_bias(rel_bias):
    delta = Q_BLOCK - _signed_offsets(2 * Q_BLOCK)
    gens = []
    for window, dil in DIL_PATTERNS:
        ok = (delta >= 0) & (delta <= window // dil)
        gens.append(jnp.where(jnp.asarray(ok)[:, None], rel_bias[_t5_bucket_np(delta * dil)], NEG_INF))
    tbl = _toeplitz(jnp.transpose(jnp.stack(gens), (0, 2, 1)).astype(F32), Q_BLOCK, 2 * Q_BLOCK)
    return [tbl[g] for g in range(len(DIL_PATTERNS))]


def _causal_bias(rel_bias, n_dist):
    heads = rel_bias.shape[1]
    dist = np.arange(n_dist)[:, None] * Q_BLOCK + _signed_offsets(Q_BLOCK)[None, :]
    vec = jnp.where(jnp.asarray(dist >= 0)[..., None], rel_bias[_t5_bucket_np(dist)], NEG_INF)
    vec = jnp.transpose(vec, (0, 2, 1)).astype(F32)
    vec = jnp.concatenate([vec, jnp.full((1, heads, TOEPLITZ_PERIOD), NEG_INF, F32)])
    return _toeplitz(vec, KEY_TILE, Q_BLOCK, side_by_side=NSA_REP, scale=LOG2_E)


def _dil_kernel(*refs, dils):
    ng = len(dils)
    groups = [refs[5 * g:5 * g + 5] for g in range(ng)]
    bias_refs = refs[5 * ng:6 * ng]
    gate_ref, y_ref = refs[6 * ng], refs[6 * ng + 1]
    ext = refs[6 * ng + 2:8 * ng + 2]
    o_acc, lse_acc = refs[8 * ng + 2:]
    first = pl.program_id(1) == 0
    scale = DIL_HEAD_DIM ** -0.5
    lane = lax.broadcasted_iota(jnp.int32, (Q_BLOCK, LANES), 1)
    low = lane < DIL_HEAD_DIM
    col = lax.broadcasted_iota(jnp.int32, (Q_BLOCK, 2 * Q_BLOCK), 1)
    halo_pen = jnp.where(col < Q_BLOCK, jnp.where(first, NEG_INF, 0.0), 0.0)
    for g, dil in enumerate(dils):
        q_ref, k_ref, kh_ref, v_ref, vh_ref = groups[g]
        kext, vext = ext[2 * g], ext[2 * g + 1]
        kext[:, 0:Q_BLOCK, :] = kh_ref[...]
        kext[:, Q_BLOCK:, :] = k_ref[...]
        vext[:, 0:Q_BLOCK, :] = vh_ref[...]
        vext[:, Q_BLOCK:, :] = v_ref[...]
        for r in range(dil):
            for j in range(k_ref.shape[1] // Q_BLOCK):
                qp = q_ref[r, j * Q_BLOCK:(j + 1) * Q_BLOCK, :]
                kb = kext[r, j * Q_BLOCK:(j + 2) * Q_BLOCK, :]
                vb = vext[r, j * Q_BLOCK:(j + 2) * Q_BLOCK, :]
                o_half, lse_half = [], []
                for half in range(2):
                    qm = jnp.where(low if half == 0 else jnp.logical_not(low), qp, jnp.zeros_like(qp))
                    sc = _dot_nt(qm, kb) * scale + bias_refs[g][half]
                    if j == 0:
                        sc = sc + halo_pen
                    m = jnp.max(sc, axis=-1, keepdims=True)
                    e = jnp.exp(sc - m)
                    l = jnp.sum(e, axis=-1, keepdims=True)
                    pv = _dot(e.astype(BF16), vb)
                    o_half.append(pv / l)
                    lse_half.append(jnp.broadcast_to(m + jnp.log(l), (Q_BLOCK, LANES)))
                o_new = jnp.where(low, o_half[0], o_half[1])
                lse_new = jnp.where(low, lse_half[0], lse_half[1])
                if dil == 1:
                    rows = pl.ds(j * Q_BLOCK, Q_BLOCK)
                else:
                    rows = pl.ds(r + dil * j * Q_BLOCK, Q_BLOCK, stride=dil)
                if g == 0:
                    o_acc[rows, :] = o_new
                    lse_acc[rows, :] = lse_new
                else:
                    o_old = o_acc[rows, :]
                    lse_old = lse_acc[rows, :]
                    top = jnp.maximum(lse_old, lse_new)
                    w_old = jnp.exp(lse_old - top)
                    w_new = jnp.exp(lse_new - top)
                    tot = w_old + w_new
                    o_acc[rows, :] = (w_old * o_old + w_new * o_new) / tot
                    if g + 1 < ng:
                        lse_acc[rows, :] = top + jnp.log(tot)
    y_ref[...] = (o_acc[...] * _silu(gate_ref[...].astype(F32))).astype(BF16)


def _dil_attention(arrays, biases, bn, s):
    dils = tuple(d for _, d in DIL_PATTERNS)
    win = DIL_WINDOW
    pairs = DIL_WIDTH // LANES
    nw = s // win
    in_specs, operands, scratch = [], [], []
    for g, dil in enumerate(dils):
        per = win // dil
        per_blocks = per // Q_BLOCK
        cur = lambda cb, dil=dil, per=per: pl.BlockSpec(
            (None, dil, per, LANES), lambda b, w, p: (b, 0, w, cb * pairs + p))
        halo = lambda cb, dil=dil, per_blocks=per_blocks: pl.BlockSpec(
            (None, dil, Q_BLOCK, LANES),
            lambda b, w, p: (b, 0, jnp.maximum(w * per_blocks - 1, 0), cb * pairs + p))
        in_specs += [cur(0), cur(1), halo(1), cur(2), halo(2)]
        operands += [arrays[g]] * 5
        scratch += [pltpu.VMEM((dil, per + Q_BLOCK, LANES), BF16)] * 2
    in_specs += [pl.BlockSpec((2, Q_BLOCK, 2 * Q_BLOCK), lambda b, w, p: (p, 0, 0))] * len(dils)
    operands += list(biases)
    in_specs.append(pl.BlockSpec((None, None, win, LANES), lambda b, w, p: (b, 0, w, 3 * pairs + p)))
    operands.append(arrays[0])
    scratch += [pltpu.VMEM((win, LANES), F32)] * 2
    return pl.pallas_call(
        functools.partial(_dil_kernel, dils=dils),
        grid=(bn, nw, pairs),
        in_specs=in_specs,
        out_specs=pl.BlockSpec((win, LANES), lambda b, w, p: (b * nw + w, p)),
        out_shape=jax.ShapeDtypeStruct((bn * s, DIL_WIDTH), BF16),
        scratch_shapes=scratch,
        compiler_params=_params("parallel", "arbitrary", "arbitrary"),
        name="dilated_attention",
    )(*operands)


def _dilated_mixer(h, g_pre, bn, s, w_in, rel_bias):
    w = DIL_WIDTH
    ng = len(DIL_PATTERNS)
    wb = w_in.astype(BF16)
    main_w = jnp.concatenate([wb[:, :3 * w], wb[:, 3 * ng * w:]], axis=1)
    arrays = [_norm_matmul(h, g_pre, main_w).reshape(bn, 1, s, 4 * w)]
    for gi in range(1, ng):
        dil = DIL_PATTERNS[gi][1]
        arrays.append(_norm_matmul(h, g_pre, wb[:, 3 * gi * w:3 * (gi + 1) * w], dil=dil, bn=bn))
    return _dil_attention(arrays, _dilated_bias(rel_bias), bn, s)


def _gelu_tanh(x):
    return 0.5 * x * (1.0 + jnp.tanh(math.sqrt(2.0 / math.pi) * (x + 0.044715 * (x * x * x))))


def _compress_kernel(x_ref, pos_ref, w1_ref, w2_ref, o_ref, *, nch):
    x = x_ref[...].astype(F32)
    top = _dot((x + pos_ref[0:1, :]).astype(BF16), w1_ref[0])
    bot = _dot((x + pos_ref[1:2, :]).astype(BF16), w1_ref[1])
    hid = top + pltpu.roll(bot, nch - 1, axis=0)
    o_ref[...] = _dot(_gelu_tanh(hid).astype(BF16), w2_ref[...]).astype(BF16)


def _compress(xkv, pos, w1, w2, nch):
    _, rows, width = xkv.shape
    hidden = w1.shape[-1]
    dh = w2.shape[-1]
    return pl.pallas_call(
        functools.partial(_compress_kernel, nch=nch),
        grid=(2, rows // nch),
        in_specs=[
            pl.BlockSpec((None, nch, width), lambda kv, i: (kv, i, 0)),
            pl.BlockSpec((None, 2, width), lambda kv, i: (kv, 0, 0)),
            pl.BlockSpec((None, 2, width, hidden), lambda kv, i: (kv, 0, 0, 0)),
            pl.BlockSpec((None, hidden, dh), lambda kv, i: (kv, 0, 0)),
        ],
        out_specs=pl.BlockSpec((None, nch, dh), lambda kv, i: (kv, i, 0)),
        out_shape=jax.ShapeDtypeStruct((2, rows, dh), BF16),
        compiler_params=_params("parallel", "parallel"),
        name="nsa_compress",
    )(xkv, pos, w1, w2)


def _bias_tile(bias_ref, d, nd):
    return bias_ref[jnp.where(d < 0, nd, jnp.minimum(d, nd - 1))]


def _nsa_front(i, q_blk, kc_ref, vc_ref, kw_ref, vw_ref, bias_ref, cov_ref, *, nsel, topk, nd):
    t0 = i * Q_BLOCK
    dh = NSA_HEAD_DIM
    rq = NSA_REP * Q_BLOCK
    scale = dh ** -0.5 * LOG2_E
    q_all = jnp.concatenate([q_blk[:, r * dh:(r + 1) * dh] for r in range(NSA_REP)], axis=0)
    qs = (q_all.astype(F32) * scale).astype(BF16)

    def bias_tile(d):
        return _bias_tile(bias_ref, d, nd)

    def per_head(x):
        return jnp.concatenate([x] * NSA_REP, axis=1)

    kc = kc_ref[...]
    nch = kc.shape[0]
    t_q = t0 + (lax.broadcasted_iota(jnp.int32, (nch, rq), 1) & (Q_BLOCK - 1))
    cend = lax.broadcasted_iota(jnp.int32, (nch, rq), 0) * CMP_STRIDE + (CMP_BLOCK - 1)
    cmask = cend <= t_q
    sc = jnp.where(cmask, _dot_nt(kc, qs), NEG_INF)
    m = jnp.max(sc, axis=0, keepdims=True)
    e = jnp.where(cmask, jnp.exp2(sc - m), 0.0)
    prob = e / jnp.maximum(jnp.sum(e, axis=0, keepdims=True), TINY)
    o_cmp = _dot_tn(vc_ref[...], prob.astype(BF16))
    psum = prob[:, 0:Q_BLOCK]
    for r in range(1, NSA_REP):
        psum = psum + prob[:, r * Q_BLOCK:(r + 1) * Q_BLOCK]
    p_hi = psum.astype(BF16)
    p_lo = (psum - p_hi.astype(F32)).astype(BF16)
    cov = cov_ref[...]
    imp = (_dot(cov, p_hi) + _dot(cov, p_lo))[0:nsel, :]
    blk = lax.broadcasted_iota(jnp.int32, (nsel, Q_BLOCK), 0)
    t_col = t0 + lax.broadcasted_iota(jnp.int32, (nsel, Q_BLOCK), 1)
    cur = t_col // SEL_BLOCK
    forced = (blk == 0) | (blk == cur) | (blk == cur - 1)
    valid = blk * SEL_BLOCK <= t_col
    imp = jnp.where(valid, imp + jnp.where(forced, FORCE_BONUS, 0.0), NEG_INF)
    ngrp = nsel // SUBLANES
    rank = [jnp.zeros((SUBLANES, Q_BLOCK), jnp.int32) for _ in range(ngrp)]
    parts = [imp[g * SUBLANES:(g + 1) * SUBLANES, :] for g in range(ngrp)]
    sub = lax.broadcasted_iota(jnp.int32, (SUBLANES, Q_BLOCK), 0)
    for mrow in range(nsel):
        other = jnp.broadcast_to(imp[mrow:mrow + 1, :], (SUBLANES, Q_BLOCK))
        for g in range(ngrp):
            ge = jnp.where(other >= parts[g], 1, 0)
            gt = jnp.where(other > parts[g], 1, 0)
            if g * SUBLANES > mrow:
                before = ge
            elif (g + 1) * SUBLANES - 1 < mrow:
                before = gt
            else:
                before = jnp.where(sub + g * SUBLANES > mrow, ge, gt)
            rank[g] = rank[g] + before
    rank = jnp.concatenate(rank, axis=0)
    sel_t = jnp.where(valid & (rank < topk), 1.0, 0.0)
    if nsel < LANES:
        sel_t = jnp.concatenate([sel_t, jnp.zeros((LANES - nsel, Q_BLOCK), F32)], axis=0)

    j0 = jnp.maximum(i - WIN // Q_BLOCK, 0)
    start = pl.multiple_of(j0 * Q_BLOCK, Q_BLOCK)
    dbase = i - j0
    sc = _dot_nt(kw_ref[pl.ds(start, WIN_SPAN), :], qs)
    dist = (dbase * Q_BLOCK + lax.broadcasted_iota(jnp.int32, (Q_BLOCK, Q_BLOCK), 1)
            - lax.broadcasted_iota(jnp.int32, (Q_BLOCK, Q_BLOCK), 0))
    too_old = per_head(jnp.where(dist <= WIN - 1, 0.0, NEG_INF))
    n_full, rem = divmod(WIN_SPAN, KEY_TILE)
    pieces = [bias_tile(dbase - (KEY_TILE // Q_BLOCK) * u) for u in range(n_full)]
    if rem:
        pieces.append(bias_tile(dbase - (KEY_TILE // Q_BLOCK) * n_full)[0:rem])
    pieces[0] = jnp.concatenate([pieces[0][0:Q_BLOCK] + too_old, pieces[0][Q_BLOCK:]], axis=0)
    sc = sc + jnp.concatenate(pieces, axis=0)
    m = jnp.max(sc, axis=0, keepdims=True)
    e = jnp.exp2(sc - m)
    l_win = jnp.sum(e, axis=0, keepdims=True)
    o_win = _dot_tn(vw_ref[pl.ds(start, WIN_SPAN), :], e.astype(BF16)) / jnp.maximum(l_win, TINY)

    sel_pen = jnp.transpose(jnp.where(sel_t > 0.5, 0.0, -MASK_BIG)).astype(BF16)
    return qs, o_cmp, o_win, sel_pen


def _nsa_kernel(q_ref, kc_ref, vc_ref, ks_ref, vs_ref, kw_ref, vw_ref, gp_ref, bg_ref,
                bias_ref, cov_ref, exp_ref, y_ref, acc_scr, *, nsel, topk, nd):
    step = pl.program_id(2)
    rq = NSA_REP * Q_BLOCK
    cols = NSA_QBLOCKS * rq
    blocks = [step * NSA_QBLOCKS + qb for qb in range(NSA_QBLOCKS)]
    fronts = [
        _nsa_front(i, q_ref[qb * Q_BLOCK:(qb + 1) * Q_BLOCK, :], kc_ref, vc_ref, kw_ref, vw_ref,
                   bias_ref, cov_ref, nsel=nsel, topk=topk, nd=nd)
        for qb, i in enumerate(blocks)]

    sub_tiles = SEL_TILE // Q_BLOCK
    q_aug = jnp.concatenate(
        [jnp.concatenate([qs, jnp.concatenate([sel_pen] * NSA_REP, axis=0)], axis=1)
         for qs, _, _, sel_pen in fronts], axis=0)
    acc_scr[...] = jnp.zeros(acc_scr.shape, F32)

    def sel_body(kt, carry):
        m_prev, l_prev = carry
        k0 = pl.multiple_of(kt * SEL_TILE, SEL_TILE)
        k_aug = jnp.concatenate([ks_ref[pl.ds(k0, SEL_TILE), :], exp_ref[kt]], axis=1)
        bias = jnp.concatenate(
            [jnp.concatenate(
                [_bias_tile(bias_ref, i - sub_tiles * kt - (KEY_TILE // Q_BLOCK) * u, nd) for i in blocks],
                axis=1)
             for u in range(SEL_TILE // KEY_TILE)], axis=0)
        sc = _dot_nt(k_aug, q_aug) + bias
        m_new = jnp.maximum(m_prev, jnp.max(sc, axis=0, keepdims=True))
        alpha = jnp.exp2(m_prev - m_new)
        e = jnp.exp2(sc - m_new)
        l_new = alpha * l_prev + jnp.sum(e, axis=0, keepdims=True)
        acc_scr[...] = alpha * acc_scr[...] + _dot_tn(vs_ref[pl.ds(k0, SEL_TILE), :], e.astype(BF16))
        return m_new, l_new

    init = (jnp.full((1, cols), NEG_INF, F32), jnp.zeros((1, cols), F32))
    _, l_sel = lax.fori_loop(0, blocks[-1] // sub_tiles + 1, sel_body, init)
    o_sel = acc_scr[...] / jnp.maximum(l_sel, TINY)
    for qb, (_, o_cmp, o_win, _) in enumerate(fronts):
        rows = slice(qb * Q_BLOCK, (qb + 1) * Q_BLOCK)
        _nsa_back(o_cmp, o_sel[:, qb * rq:(qb + 1) * rq], o_win, bg_ref[rows, :], gp_ref, y_ref, rows)


def _nsa_back(o_cmp, o_sel, o_win, branch_gates, gp_ref, y_ref, rows):
    dh = NSA_HEAD_DIM
    gates_t = jnp.transpose(_sigmoid(branch_gates.astype(F32)))

    def gate_row(branch):
        return jnp.concatenate(
            [gates_t[3 * r + branch:3 * r + branch + 1, :] for r in range(NSA_REP)], axis=1)

    o_t = gate_row(0) * o_cmp + gate_row(1) * o_sel + gate_row(2) * o_win
    for r in range(NSA_REP):
        o = jnp.transpose(o_t[:, r * Q_BLOCK:(r + 1) * Q_BLOCK])
        gp = gp_ref[rows, r * dh:(r + 1) * dh].astype(F32)
        y_ref[rows, r * dh:(r + 1) * dh] = (o * _silu(gp)).astype(BF16)


def _nsa_layout(s):
    nch = s // CMP_STRIDE
    nc = (s - CMP_BLOCK) // CMP_STRIDE + 1
    nsel = s // SEL_BLOCK
    start = np.arange(nc) * CMP_STRIDE
    sel_start = np.arange(nsel) * SEL_BLOCK
    cover = (start[:, None] < sel_start[None, :] + SEL_BLOCK) & (
        start[:, None] + CMP_BLOCK - 1 >= sel_start[None, :])
    cov_t = np.zeros((LANES, nch), np.float32)
    cov_t[:nsel, :nc] = cover.T
    key_blk = np.arange(s) // SEL_BLOCK
    expand = (key_blk[:, None] == np.arange(LANES)[None, :]).astype(np.float32)
    expand = expand.reshape(s // SEL_TILE, SEL_TILE, LANES)
    return jnp.asarray(cov_t, BF16), jnp.asarray(expand, BF16)


def _nsa_attention(proj_a, proj_b, cmp_kv, bias, bn, s):
    t = proj_a.shape[0]
    g_n, dh, rep = NSA_KV_GROUPS, NSA_HEAD_DIM, NSA_REP
    gw = rep * dh
    nq = s // Q_BLOCK
    nch = s // CMP_STRIDE
    nsel = s // SEL_BLOCK
    nd = bias.shape[0] - 1
    cov_t, expand = _nsa_layout(s)
    ksel0 = (NSA_WIDTH + 2 * NSA_KV_WIDTH) // dh
    vsel0 = 0
    kwin0 = NSA_KV_WIDTH // dh
    vwin0 = 2 * NSA_KV_WIDTH // dh
    kcol = lambda c0: pl.BlockSpec((s, dh), lambda b, g, i: (b, c0 + g))
    gp0 = 3 * NSA_KV_WIDTH // gw
    bg0 = (3 * NSA_KV_WIDTH + NSA_WIDTH) // LANES
    kern = functools.partial(_nsa_kernel, nsel=nsel, topk=min(SEL_TOPK, nsel), nd=nd)
    steps = nq // NSA_QBLOCKS
    rows = NSA_QBLOCKS * Q_BLOCK
    return pl.pallas_call(
        kern,
        grid=(bn, g_n, steps),
        in_specs=[
            pl.BlockSpec((rows, gw), lambda b, g, i: (b * steps + i, g)),
            pl.BlockSpec((None, nch, dh), lambda b, g, i: (0, b * g_n + g, 0)),
            pl.BlockSpec((None, nch, dh), lambda b, g, i: (1, b * g_n + g, 0)),
            kcol(ksel0), kcol(vsel0), kcol(kwin0), kcol(vwin0),
            pl.BlockSpec((rows, gw), lambda b, g, i: (b * steps + i, gp0 + g)),
            pl.BlockSpec((rows, LANES), lambda b, g, i: (b * steps + i, bg0 + g)),
            pl.BlockSpec((nd + 1, None, KEY_TILE, rep * Q_BLOCK), lambda b, g, i: (0, g, 0, 0),
                         pipeline_mode=pl.Buffered(1)),
            _resident(cov_t.shape, lambda b, g, i: (0, 0)),
            _resident(expand.shape, lambda b, g, i: (0, 0, 0)),
        ],
        out_specs=pl.BlockSpec((rows, gw), lambda b, g, i: (b * steps + i, g)),
        out_shape=jax.ShapeDtypeStruct((t, NSA_WIDTH), BF16),
        scratch_shapes=[pltpu.VMEM((dh, NSA_QBLOCKS * rep * Q_BLOCK), F32)],
        compiler_params=_params("parallel", "parallel", "arbitrary"),
        name="nsa_attention",
    )(proj_a, cmp_kv, cmp_kv, proj_a, proj_b, proj_b, proj_b, proj_b, proj_b, bias, cov_t, expand)


def _nsa_in_weights(w_in):
    split = NSA_WIDTH + 3 * NSA_KV_WIDTH
    main = NSA_WIDTH + 6 * NSA_KV_WIDTH
    ngate = 3 * NSA_HEADS
    per_group = ngate // NSA_KV_GROUPS
    d = w_in.shape[0]
    bg = w_in[:, main:main + ngate].reshape(d, NSA_KV_GROUPS, per_group)
    bg = jnp.pad(bg, ((0, 0), (0, 0), (0, LANES - per_group))).reshape(d, NSA_KV_GROUPS * LANES)
    w_b = jnp.concatenate([w_in[:, split:main], w_in[:, main + ngate:], bg], axis=1)
    return w_in[:, :split].astype(BF16), w_b.astype(BF16)


def _nsa_mixer(h, g_pre, bn, s, w_in, pos_k, w1_k, w2_k, pos_v, w1_v, w2_v, rel_bias):
    g_n, dh = NSA_KV_GROUPS, NSA_HEAD_DIM
    w_a, w_b = _nsa_in_weights(w_in)
    proj_a = _norm_matmul(h, g_pre, w_a)
    proj_b = _norm_matmul(h, g_pre, w_b)
    nch = s // CMP_STRIDE
    p3 = proj_a.reshape(bn, nch, CMP_STRIDE, proj_a.shape[1])

    def chunks(c0):
        x = p3[..., c0:c0 + NSA_KV_WIDTH].reshape(bn, nch, CMP_STRIDE, g_n, dh)
        return jnp.transpose(x, (0, 3, 1, 2, 4)).reshape(bn * g_n * nch, CMP_STRIDE * dh)

    xkv = jnp.stack([chunks(NSA_WIDTH), chunks(NSA_WIDTH + NSA_KV_WIDTH)])
    half = CMP_STRIDE * dh
    pos = jnp.stack([pos_k.reshape(2, half), pos_v.reshape(2, half)])
    w1 = jnp.stack([w1_k.reshape(2, half, -1), w1_v.reshape(2, half, -1)]).astype(BF16)
    w2 = jnp.stack([w2_k, w2_v]).astype(BF16)
    cmp_kv = _compress(xkv, pos, w1, w2, nch)
    nd = min(s // Q_BLOCK, 15)
    bias = _causal_bias(rel_bias, nd)
    return _nsa_attention(proj_a, proj_b, cmp_kv, bias, bn, s)


def kernel(x, p, rel_bias, norm_pre, norm_post, ple_w_proj, ple_w_gate, a_w_in, a_conv_w, a_conv_b,
           a_w_r, a_b_r, a_w_i, a_b_i, a_lam, a_w_out, b_w_in, b_w_out, c_w_in, c_cmp_pos_k,
           c_cmp_w1_k, c_cmp_w2_k, c_cmp_pos_v, c_cmp_w1_v, c_cmp_w2_v, c_w_out):
    bn, s, d = x.shape
    depth = p.shape[0]
    t = bn * s
    h = x.reshape(t, d)
    for i in range(depth):
        kind = i % N_MIXERS
        j = i // N_MIXERS
        if kind == 0:
            proj = _norm_matmul(h, norm_pre[i], a_w_in[j].astype(BF16))
            y = _rglru(proj, bn, s, a_conv_w[j], a_conv_b[j], a_w_r[j], a_b_r[j], a_w_i[j],
                       a_b_i[j], a_lam[j])
            w_out = a_w_out[j]
        elif kind == 1:
            y = _dilated_mixer(h, norm_pre[i], bn, s, b_w_in[j], rel_bias)
            w_out = b_w_out[j]
        else:
            y = _nsa_mixer(h, norm_pre[i], bn, s, c_w_in[j], c_cmp_pos_k[j], c_cmp_w1_k[j],
                           c_cmp_w2_k[j], c_cmp_pos_v[j], c_cmp_w1_v[j], c_cmp_w2_v[j], rel_bias)
            w_out = c_w_out[j]
        h = _out_ple(y, w_out.astype(BF16), norm_post[i], h, p[i].reshape(t, -1),
                     ple_w_proj[i].astype(BF16), ple_w_gate[i].astype(BF16))
    return h.reshape(bn, s, d)
```

```python
import functools
import math

import numpy as np
import jax
import jax.numpy as jnp
from jax import lax
from jax.experimental import pallas as pl
from jax.experimental.pallas import tpu as pltpu

F32 = jnp.float32
BF16 = jnp.bfloat16

NORM_EPS = 1e-6
NEG_INF = -1e30
TINY = 1e-30
MASK_BIG = 2.0 ** 40
LOG2_E = math.log2(math.e)

N_MIXERS = 3
N_BUCKETS = 32
BUCKET_MAX_DIST = 2048

LRU_BLOCKS = 8
CONV_W = 4
LRU_C = 8.0

DIL_PATTERNS = ((128, 1), (512, 4), (2048, 16))
DIL_HEADS = 16
DIL_HEAD_DIM = 64
DIL_WIDTH = DIL_HEADS * DIL_HEAD_DIM

NSA_HEADS = 16
NSA_KV_GROUPS = 4
NSA_REP = NSA_HEADS // NSA_KV_GROUPS
NSA_HEAD_DIM = 128
NSA_WIDTH = NSA_HEADS * NSA_HEAD_DIM
NSA_KV_WIDTH = NSA_KV_GROUPS * NSA_HEAD_DIM
CMP_BLOCK = 32
CMP_STRIDE = 16
SEL_BLOCK = 64
SEL_TOPK = 16
WIN = 512
FORCE_BONUS = 1e4

SUBLANES = 8
LANES = 128
Q_BLOCK = 128
KEY_TILE = 256
DIL_WINDOW = Q_BLOCK * max(d for _, d in DIL_PATTERNS)
SEL_TILE = 1024
NSA_QBLOCKS = 4
PROJ_ROWS = 512
PROJ_HALVES = 2
WIN_SPAN = WIN + Q_BLOCK
VMEM_LIMIT_BYTES = 56 * 1024 * 1024


def _params(*sem):
    return pltpu.CompilerParams(dimension_semantics=sem, vmem_limit_bytes=VMEM_LIMIT_BYTES)


def _dot(a, b):
    return jnp.dot(a, b, preferred_element_type=F32)


def _dot_nt(a, b):
    return lax.dot_general(a, b, (((1,), (1,)), ((), ())), preferred_element_type=F32)


def _dot_tn(a, b):
    return lax.dot_general(a, b, (((0,), (0,)), ((), ())), preferred_element_type=F32)


def _sigmoid(x):
    return 1.0 / (1.0 + jnp.exp(-x))


def _silu(x):
    return x * _sigmoid(x)


def _resident(block_shape, index_map):
    return pl.BlockSpec(block_shape, index_map, pipeline_mode=pl.Buffered(1))


def _norm_matmul_kernel(h_ref, g_ref, w_ref, o_ref, *scratch, dil):
    n = w_ref.shape[1]
    half_rows = h_ref.shape[0] // PROJ_HALVES
    for hh in range(PROJ_HALVES):
        x = h_ref[hh * half_rows:(hh + 1) * half_rows, :]
        ms = jnp.mean(x * x, axis=-1, keepdims=True)
        u = (x * lax.rsqrt(ms + NORM_EPS) * g_ref[...]).astype(BF16)
        res = _dot(u, w_ref[...])
        if dil == 1:
            o_ref[hh * half_rows:(hh + 1) * half_rows, :] = res.astype(o_ref.dtype)
            continue
        res_scr = scratch[0]
        per = half_rows // dil
        for c in range(n // LANES):
            res_scr[hh, c] = res[:, c * LANES:(c + 1) * LANES]
        for r in range(dil):
            for c in range(n // LANES):
                o_ref[r, hh * per:(hh + 1) * per, c * LANES:(c + 1) * LANES] = (
                    res_scr[hh, c, pl.ds(r, per, stride=dil), :].astype(o_ref.dtype))


def _norm_matmul(h, g, w, dil=1, bn=1):
    t, d = h.shape
    n = w.shape[1]
    tm = min(t, PROJ_ROWS)
    in_specs = [
        pl.BlockSpec((tm, d), lambda i: (i, 0)),
        _resident((1, d), lambda i: (0, 0)),
        _resident((d, n), lambda i: (0, 0)),
    ]
    if dil > 1:
        s = t // bn
        nt = s // tm
        out_spec = pl.BlockSpec((None, dil, tm // dil, n), lambda i: (i // nt, 0, i % nt, 0))
        out_shape = jax.ShapeDtypeStruct((bn, dil, s // dil, n), BF16)
        scratch = [pltpu.VMEM((PROJ_HALVES, n // LANES, tm // PROJ_HALVES, LANES), F32)]
    else:
        out_spec = pl.BlockSpec((tm, n), lambda i: (i, 0))
        out_shape = jax.ShapeDtypeStruct((t, n), BF16)
        scratch = []
    return pl.pallas_call(
        functools.partial(_norm_matmul_kernel, dil=dil),
        grid=(t // tm,),
        in_specs=in_specs,
        out_specs=out_spec,
        out_shape=out_shape,
        scratch_shapes=scratch,
        compiler_params=_params("parallel"),
        name="norm_matmul" if dil == 1 else f"norm_matmul_residue_{dil}",
    )(h, g.reshape(1, d), w)


def _out_ple_kernel(y_ref, wo_ref, g_ref, h_ref, p_ref, wp_ref, wg_ref, o_ref):
    z = _dot(y_ref[...], wo_ref[...])
    ms = jnp.mean(z * z, axis=-1, keepdims=True)
    h1 = h_ref[...] + z * lax.rsqrt(ms + NORM_EPS) * g_ref[...]
    gate = _sigmoid(_dot(h1.astype(BF16), wg_ref[...]))
    pe = _dot(p_ref[...].astype(BF16), wp_ref[...])
    o_ref[...] = h1 + gate * pe


def _out_ple(y, w_out, g_post, h, p, w_proj, w_gate):
    t, d = h.shape
    k = y.shape[1]
    pd = p.shape[1]
    tm = min(t, 256)
    return pl.pallas_call(
        _out_ple_kernel,
        grid=(t // tm,),
        in_specs=[
            pl.BlockSpec((tm, k), lambda i: (i, 0)),
            _resident((k, d), lambda i: (0, 0)),
            _resident((1, d), lambda i: (0, 0)),
            pl.BlockSpec((tm, d), lambda i: (i, 0)),
            pl.BlockSpec((tm, pd), lambda i: (i, 0)),
            _resident((pd, d), lambda i: (0, 0)),
            _resident((d, d), lambda i: (0, 0)),
        ],
        out_specs=pl.BlockSpec((tm, d), lambda i: (i, 0)),
        out_shape=jax.ShapeDtypeStruct((t, d), F32),
        compiler_params=_params("parallel"),
        name="out_ple",
    )(y, w_out, g_post.reshape(1, d), h, p, w_proj, w_gate)


def _lru_kernel(xb_ref, gate_ref, cw_ref, cb_ref, wr_ref, br_ref, wi_ref, bi_ref, lam_ref,
                y_ref, xext_scr, h_scr, a_scr, b_scr, h_scan, *, ts, bw):
    @pl.when(pl.program_id(1) == 0)
    def _():
        xext_scr[0:SUBLANES, :] = jnp.zeros((SUBLANES, xext_scr.shape[1]), F32)
        h_scr[...] = jnp.zeros(h_scr.shape, F32)

    seg = ts // SUBLANES
    pitch = seg + SUBLANES
    row = lax.broadcasted_iota(jnp.int32, (SUBLANES, LANES), 0)
    for n in range(LRU_BLOCKS):
        cols = slice(n * bw, (n + 1) * bw)
        x = xb_ref[:, cols].astype(F32)
        xext_scr[SUBLANES:SUBLANES + ts, cols] = x
        xc = cw_ref[3:4, cols] * x + cb_ref[:, cols]
        for k in range(1, CONV_W):
            xc = xc + cw_ref[3 - k:4 - k, cols] * xext_scr[SUBLANES - k:SUBLANES - k + ts, cols]
        xext_scr[0:SUBLANES, cols] = x[ts - SUBLANES:ts, :]

        xn = xc.astype(BF16)
        r = _sigmoid(_dot(xn, wr_ref[n]) + br_ref[:, cols])
        ig = _sigmoid(_dot(xn, wi_ref[n]) + bi_ref[:, cols])
        nl = -lam_ref[:, cols]
        softplus = jnp.maximum(nl, 0.0) + jnp.log(1.0 + jnp.exp(-jnp.abs(nl)))
        a = jnp.exp2(r * ((-LRU_C * LOG2_E) * softplus))
        b = jnp.sqrt(1.0 - a * a) * (ig * xc)

        for sl in range(bw // LANES):
            lanes = slice(n * bw + sl * LANES, n * bw + (sl + 1) * LANES)
            slab = n * (bw // LANES) + sl
            for sg in range(SUBLANES):
                a_scr[slab, sg * pitch:sg * pitch + seg, :] = a[sg * seg:(sg + 1) * seg, sl * LANES:(sl + 1) * LANES]
                b_scr[slab, sg * pitch:sg * pitch + seg, :] = b[sg * seg:(sg + 1) * seg, sl * LANES:(sl + 1) * LANES]
            h_loc = jnp.zeros((SUBLANES, LANES), F32)
            a_loc = jnp.ones((SUBLANES, LANES), F32)
            h_steps, a_steps = [], []
            for j in range(seg):
                a_j = a_scr[slab, pl.ds(j, SUBLANES, stride=pitch), :]
                b_j = b_scr[slab, pl.ds(j, SUBLANES, stride=pitch), :]
                h_loc = a_j * h_loc + b_j
                a_loc = a_j * a_loc
                h_steps.append(h_loc)
                a_steps.append(a_loc)
            a_tot, h_tot = a_loc, h_loc
            for dist in (1, 2, 4):
                a_sh = jnp.where(row >= dist, pltpu.roll(a_tot, dist, axis=0), 1.0)
                h_sh = jnp.where(row >= dist, pltpu.roll(h_tot, dist, axis=0), 0.0)
                h_tot = a_tot * h_sh + h_tot
                a_tot = a_tot * a_sh
            h_in = h_scr[:, lanes]
            h_end = a_tot * h_in + h_tot
            h_enter = jnp.where(row >= 1, pltpu.roll(h_end, 1, axis=0), h_in)
            for j in range(seg):
                h_scan[slab, pl.ds(j, SUBLANES, stride=pitch), :] = h_steps[j] + a_steps[j] * h_enter
            h_scr[:, lanes] = jnp.broadcast_to(h_end[SUBLANES - 1:SUBLANES, :], (SUBLANES, LANES))
            hs = jnp.concatenate(
                [h_scan[slab, sg * pitch:sg * pitch + seg, :] for sg in range(SUBLANES)], axis=0)
            y_ref[:, lanes] = (hs * _silu(gate_ref[:, lanes].astype(F32))).astype(BF16)


def _rglru(proj, bn, s, conv_w, conv_b, w_r, b_r, w_i, b_i, lam):
    t, two_c = proj.shape
    c = two_c // 2
    bw = c // LRU_BLOCKS
    ts = min(s, 256)
    nst = s // ts
    row = lambda v: v.reshape(1, c)
    kern = functools.partial(_lru_kernel, ts=ts, bw=bw)
    return pl.pallas_call(
        kern,
        grid=(bn, nst),
        in_specs=[
            pl.BlockSpec((ts, c), lambda b, i: (b * nst + i, 0)),
            pl.BlockSpec((ts, c), lambda b, i: (b * nst + i, 1)),
            pl.BlockSpec((CONV_W, c), lambda b, i: (0, 0)),
            pl.BlockSpec((1, c), lambda b, i: (0, 0)),
            pl.BlockSpec((LRU_BLOCKS, bw, bw), lambda b, i: (0, 0, 0)),
            pl.BlockSpec((1, c), lambda b, i: (0, 0)),
            pl.BlockSpec((LRU_BLOCKS, bw, bw), lambda b, i: (0, 0, 0)),
            pl.BlockSpec((1, c), lambda b, i: (0, 0)),
            pl.BlockSpec((1, c), lambda b, i: (0, 0)),
        ],
        out_specs=pl.BlockSpec((ts, c), lambda b, i: (b * nst + i, 0)),
        out_shape=jax.ShapeDtypeStruct((t, c), BF16),
        scratch_shapes=[pltpu.VMEM((ts + SUBLANES, c), F32), pltpu.VMEM((SUBLANES, c), F32)]
        + [pltpu.VMEM((c // LANES, ts + SUBLANES * SUBLANES, LANES), F32)] * 3,
        compiler_params=_params("parallel", "arbitrary"),
        name="rglru",
    )(proj, proj, conv_w, row(conv_b), w_r.astype(BF16), row(b_r), w_i.astype(BF16), row(b_i),
      row(lam))


def _t5_bucket_np(dist):
    n = np.maximum(dist, 0)
    max_exact = N_BUCKETS // 2
    nf = np.maximum(n, max_exact).astype(np.float64)
    large = max_exact + (np.log(nf / max_exact) / math.log(BUCKET_MAX_DIST / max_exact)
                         * (N_BUCKETS - max_exact)).astype(np.int64)
    return np.where(n < max_exact, n, np.minimum(large, N_BUCKETS - 1)).astype(np.int32)


TOEPLITZ_PERIOD = Q_BLOCK + KEY_TILE


def _toeplitz_kernel(w_ref, o_ref, *, rows, cols, side_by_side, scale):
    for h in range(w_ref.shape[0]):
        gen = jnp.broadcast_to(w_ref[h:h + 1, :] * scale, (rows, w_ref.shape[1]))
        tile = pltpu.roll(gen, 0, 1, stride=1, stride_axis=0)[:, :cols]
        o_ref[h // side_by_side, :, (h % side_by_side) * cols:(h % side_by_side + 1) * cols] = tile


def _toeplitz(w, rows, cols, side_by_side=1, scale=1.0):
    n, heads, period = w.shape
    out_block = (None, heads // side_by_side, rows, side_by_side * cols)
    return pl.pallas_call(
        functools.partial(_toeplitz_kernel, rows=rows, cols=cols, side_by_side=side_by_side, scale=scale),
        grid=(n,),
        in_specs=[pl.BlockSpec((None, heads, period), lambda i: (i, 0, 0))],
        out_specs=pl.BlockSpec(out_block, lambda i: (i, 0, 0, 0)),
        out_shape=jax.ShapeDtypeStruct((n,) + out_block[1:], F32),
        compiler_params=_params("parallel"),
        name="toeplitz_bias",
    )(w)


def _signed_offsets(n_cols):
    m = np.arange(TOEPLITZ_PERIOD)
    return np.where(m < n_cols, m, m - TOEPLITZ_PERIOD)


def _dilated_bias(rel_bias):
    delta = Q_BLOCK - _signed_offsets(2 * Q_BLOCK)
    gens = []
    for window, dil in DIL_PATTERNS:
        ok = (delta >= 0) & (delta <= window // dil)
        gens.append(jnp.where(jnp.asarray(ok)[:, None], rel_bias[_t5_bucket_np(delta * dil)], NEG_INF))
    tbl = _toeplitz(jnp.transpose(jnp.stack(gens), (0, 2, 1)).astype(F32), Q_BLOCK, 2 * Q_BLOCK,
                    scale=LOG2_E)
    return [tbl[g] for g in range(len(DIL_PATTERNS))]


def _causal_bias(rel_bias, n_dist):
    heads = rel_bias.shape[1]
    dist = np.arange(n_dist)[:, None] * Q_BLOCK + _signed_offsets(Q_BLOCK)[None, :]
    vec = jnp.where(jnp.asarray(dist >= 0)[..., None], rel_bias[_t5_bucket_np(dist)], NEG_INF)
    vec = jnp.transpose(vec, (0, 2, 1)).astype(F32)
    vec = jnp.concatenate([vec, jnp.full((1, heads, TOEPLITZ_PERIOD), NEG_INF, F32)])
    return _toeplitz(vec, KEY_TILE, Q_BLOCK, side_by_side=NSA_REP, scale=LOG2_E)


def _dil_kernel(*refs, dils):
    ng = len(dils)
    groups = [refs[5 * g:5 * g + 5] for g in range(ng)]
    bias_refs = refs[5 * ng:6 * ng]
    gate_ref, y_ref = refs[6 * ng], refs[6 * ng + 1]
    ext = refs[6 * ng + 2:8 * ng + 2]
    o_acc, lse_acc = refs[8 * ng + 2:]
    first = pl.program_id(1) == 0
    scale = DIL_HEAD_DIM ** -0.5 * LOG2_E
    lane = lax.broadcasted_iota(jnp.int32, (Q_BLOCK, LANES), 1)
    low = lane < DIL_HEAD_DIM
    col = lax.broadcasted_iota(jnp.int32, (Q_BLOCK, 2 * Q_BLOCK), 1)
    halo_pen = jnp.where(col < Q_BLOCK, jnp.where(first, NEG_INF, 0.0), 0.0)
    for g, dil in enumerate(dils):
        q_ref, k_ref, kh_ref, v_ref, vh_ref = groups[g]
        kext, vext = ext[2 * g], ext[2 * g + 1]
        kext[:, 0:Q_BLOCK, :] = kh_ref[...]
        kext[:, Q_BLOCK:, :] = k_ref[...]
        vext[:, 0:Q_BLOCK, :] = vh_ref[...]
        vext[:, Q_BLOCK:, :] = v_ref[...]
        for r in range(dil):
            for j in range(k_ref.shape[1] // Q_BLOCK):
                qp = q_ref[r, j * Q_BLOCK:(j + 1) * Q_BLOCK, :]
                kb = kext[r, j * Q_BLOCK:(j + 2) * Q_BLOCK, :]
                vb = vext[r, j * Q_BLOCK:(j + 2) * Q_BLOCK, :]
                o_half, lse_half = [], []
                for half in range(2):
                    qm = jnp.where(low if half == 0 else jnp.logical_not(low), qp, jnp.zeros_like(qp))
                    sc = _dot_nt(qm, kb) * scale + bias_refs[g][half]
                    if j == 0:
                        sc = sc + halo_pen
                    m = jnp.max(sc, axis=-1, keepdims=True)
                    e = jnp.exp2(sc - m)
                    l = jnp.sum(e, axis=-1, keepdims=True)
                    pv = _dot(e.astype(BF16), vb)
                    o_half.append(pv / l)
                    lse_half.append(jnp.broadcast_to(m + jnp.log2(l), (Q_BLOCK, LANES)))
                o_new = jnp.where(low, o_half[0], o_half[1])
                lse_new = jnp.where(low, lse_half[0], lse_half[1])
                if dil == 1:
                    rows = pl.ds(j * Q_BLOCK, Q_BLOCK)
                else:
                    rows = pl.ds(r + dil * j * Q_BLOCK, Q_BLOCK, stride=dil)
                if g == 0:
                    o_acc[rows, :] = o_new
                    lse_acc[rows, :] = lse_new
                else:
                    o_old = o_acc[rows, :]
                    lse_old = lse_acc[rows, :]
                    top = jnp.maximum(lse_old, lse_new)
                    w_old = jnp.exp2(lse_old - top)
                    w_new = jnp.exp2(lse_new - top)
                    tot = w_old + w_new
                    o_acc[rows, :] = (w_old * o_old + w_new * o_new) / tot
                    if g + 1 < ng:
                        lse_acc[rows, :] = top + jnp.log2(tot)
    y_ref[...] = (o_acc[...] * _silu(gate_ref[...].astype(F32))).astype(BF16)


def _dil_attention(arrays, biases, bn, s):
    dils = tuple(d for _, d in DIL_PATTERNS)
    win = DIL_WINDOW
    pairs = DIL_WIDTH // LANES
    nw = s // win
    in_specs, operands, scratch = [], [], []
    for g, dil in enumerate(dils):
        per = win // dil
        per_blocks = per // Q_BLOCK
        cur = lambda cb, dil=dil, per=per: pl.BlockSpec(
            (None, dil, per, LANES), lambda b, w, p: (b, 0, w, cb * pairs + p))
        halo = lambda cb, dil=dil, per_blocks=per_blocks: pl.BlockSpec(
            (None, dil, Q_BLOCK, LANES),
            lambda b, w, p: (b, 0, jnp.maximum(w * per_blocks - 1, 0), cb * pairs + p))
        in_specs += [cur(0), cur(1), halo(1), cur(2), halo(2)]
        operands += [arrays[g]] * 5
        scratch += [pltpu.VMEM((dil, per + Q_BLOCK, LANES), BF16)] * 2
    in_specs += [pl.BlockSpec((2, Q_BLOCK, 2 * Q_BLOCK), lambda b, w, p: (p, 0, 0))] * len(dils)
    operands += list(biases)
    in_specs.append(pl.BlockSpec((None, None, win, LANES), lambda b, w, p: (b, 0, w, 3 * pairs + p)))
    operands.append(arrays[0])
    scratch += [pltpu.VMEM((win, LANES), F32)] * 2
    return pl.pallas_call(
        functools.partial(_dil_kernel, dils=dils),
        grid=(bn, nw, pairs),
        in_specs=in_specs,
        out_specs=pl.BlockSpec((win, LANES), lambda b, w, p: (b * nw + w, p)),
        out_shape=jax.ShapeDtypeStruct((bn * s, DIL_WIDTH), BF16),
        scratch_shapes=scratch,
        compiler_params=_params("parallel", "arbitrary", "arbitrary"),
        name="dilated_attention",
    )(*operands)


def _dilated_mixer(h, g_pre, bn, s, w_in, rel_bias):
    w = DIL_WIDTH
    ng = len(DIL_PATTERNS)
    wb = w_in.astype(BF16)
    main_w = jnp.concatenate([wb[:, :3 * w], wb[:, 3 * ng * w:]], axis=1)
    arrays = [_norm_matmul(h, g_pre, main_w).reshape(bn, 1, s, 4 * w)]
    for gi in range(1, ng):
        dil = DIL_PATTERNS[gi][1]
        arrays.append(_norm_matmul(h, g_pre, wb[:, 3 * gi * w:3 * (gi + 1) * w], dil=dil, bn=bn))
    return _dil_attention(arrays, _dilated_bias(rel_bias), bn, s)


def _gelu_tanh(x):
    return 0.5 * x * (1.0 + jnp.tanh(math.sqrt(2.0 / math.pi) * (x + 0.044715 * (x * x * x))))


def _compress_kernel(x_ref, pos_ref, w1_ref, w2_ref, o_ref, *, nch):
    x = x_ref[...].astype(F32)
    top = _dot((x + pos_ref[0:1, :]).astype(BF16), w1_ref[0])
    bot = _dot((x + pos_ref[1:2, :]).astype(BF16), w1_ref[1])
    hid = top + pltpu.roll(bot, nch - 1, axis=0)
    o_ref[...] = _dot(_gelu_tanh(hid).astype(BF16), w2_ref[...]).astype(BF16)


def _compress(xkv, pos, w1, w2, nch):
    _, rows, width = xkv.shape
    hidden = w1.shape[-1]
    dh = w2.shape[-1]
    return pl.pallas_call(
        functools.partial(_compress_kernel, nch=nch),
        grid=(2, rows // nch),
        in_specs=[
            pl.BlockSpec((None, nch, width), lambda kv, i: (kv, i, 0)),
            pl.BlockSpec((None, 2, width), lambda kv, i: (kv, 0, 0)),
            pl.BlockSpec((None, 2, width, hidden), lambda kv, i: (kv, 0, 0, 0)),
            pl.BlockSpec((None, hidden, dh), lambda kv, i: (kv, 0, 0)),
        ],
        out_specs=pl.BlockSpec((None, nch, dh), lambda kv, i: (kv, i, 0)),
        out_shape=jax.ShapeDtypeStruct((2, rows, dh), BF16),
        compiler_params=_params("parallel", "parallel"),
        name="nsa_compress",
    )(xkv, pos, w1, w2)


def _bias_tile(bias_ref, d, nd):
    return bias_ref[jnp.where(d < 0, nd, jnp.minimum(d, nd - 1))]


def _nsa_front(i, q_blk, kc_ref, vc_ref, kw_ref, vw_ref, bias_ref, cov_ref, *, nsel, topk, nd):
    t0 = i * Q_BLOCK
    dh = NSA_HEAD_DIM
    rq = NSA_REP * Q_BLOCK
    scale = dh ** -0.5 * LOG2_E
    q_all = jnp.concatenate([q_blk[:, r * dh:(r + 1) * dh] for r in range(NSA_REP)], axis=0)
    qs = (q_all.astype(F32) * scale).astype(BF16)

    def bias_tile(d):
        return _bias_tile(bias_ref, d, nd)

    def per_head(x):
        return jnp.concatenate([x] * NSA_REP, axis=1)

    kc = kc_ref[...]
    nch = kc.shape[0]
    t_q = t0 + (lax.broadcasted_iota(jnp.int32, (nch, rq), 1) & (Q_BLOCK - 1))
    cend = lax.broadcasted_iota(jnp.int32, (nch, rq), 0) * CMP_STRIDE + (CMP_BLOCK - 1)
    cmask = cend <= t_q
    sc = jnp.where(cmask, _dot_nt(kc, qs), NEG_INF)
    m = jnp.max(sc, axis=0, keepdims=True)
    e = jnp.where(cmask, jnp.exp2(sc - m), 0.0)
    prob = e / jnp.maximum(jnp.sum(e, axis=0, keepdims=True), TINY)
    o_cmp = _dot_tn(vc_ref[...], prob.astype(BF16))
    psum = prob[:, 0:Q_BLOCK]
    for r in range(1, NSA_REP):
        psum = psum + prob[:, r * Q_BLOCK:(r + 1) * Q_BLOCK]
    p_hi = psum.astype(BF16)
    p_lo = (psum - p_hi.astype(F32)).astype(BF16)
    cov = cov_ref[...]
    imp = (_dot(cov, p_hi) + _dot(cov, p_lo))[0:nsel, :]
    blk = lax.broadcasted_iota(jnp.int32, (nsel, Q_BLOCK), 0)
    t_col = t0 + lax.broadcasted_iota(jnp.int32, (nsel, Q_BLOCK), 1)
    cur = t_col // SEL_BLOCK
    forced = (blk == 0) | (blk == cur) | (blk == cur - 1)
    valid = blk * SEL_BLOCK <= t_col
    imp = jnp.where(valid, imp + jnp.where(forced, FORCE_BONUS, 0.0), NEG_INF)
    ngrp = nsel // SUBLANES
    rank = [jnp.zeros((SUBLANES, Q_BLOCK), jnp.int32) for _ in range(ngrp)]
    parts = [imp[g * SUBLANES:(g + 1) * SUBLANES, :] for g in range(ngrp)]
    sub = lax.broadcasted_iota(jnp.int32, (SUBLANES, Q_BLOCK), 0)
    for mrow in range(nsel):
        other = jnp.broadcast_to(imp[mrow:mrow + 1, :], (SUBLANES, Q_BLOCK))
        for g in range(ngrp):
            ge = jnp.where(other >= parts[g], 1, 0)
            gt = jnp.where(other > parts[g], 1, 0)
            if g * SUBLANES > mrow:
                before = ge
            elif (g + 1) * SUBLANES - 1 < mrow:
                before = gt
            else:
                before = jnp.where(sub + g * SUBLANES > mrow, ge, gt)
            rank[g] = rank[g] + before
    rank = jnp.concatenate(rank, axis=0)
    sel_t = jnp.where(valid & (rank < topk), 1.0, 0.0)
    if nsel < LANES:
        sel_t = jnp.concatenate([sel_t, jnp.zeros((LANES - nsel, Q_BLOCK), F32)], axis=0)

    j0 = jnp.maximum(i - WIN // Q_BLOCK, 0)
    start = pl.multiple_of(j0 * Q_BLOCK, Q_BLOCK)
    dbase = i - j0
    sc = _dot_nt(kw_ref[pl.ds(start, WIN_SPAN), :], qs)
    dist = (dbase * Q_BLOCK + lax.broadcasted_iota(jnp.int32, (Q_BLOCK, Q_BLOCK), 1)
            - lax.broadcasted_iota(jnp.int32, (Q_BLOCK, Q_BLOCK), 0))
    too_old = per_head(jnp.where(dist <= WIN - 1, 0.0, NEG_INF))
    n_full, rem = divmod(WIN_SPAN, KEY_TILE)
    pieces = [bias_tile(dbase - (KEY_TILE // Q_BLOCK) * u) for u in range(n_full)]
    if rem:
        pieces.append(bias_tile(dbase - (KEY_TILE // Q_BLOCK) * n_full)[0:rem])
    pieces[0] = jnp.concatenate([pieces[0][0:Q_BLOCK] + too_old, pieces[0][Q_BLOCK:]], axis=0)
    sc = sc + jnp.concatenate(pieces, axis=0)
    m = jnp.max(sc, axis=0, keepdims=True)
    e = jnp.exp2(sc - m)
    l_win = jnp.sum(e, axis=0, keepdims=True)
    o_win = _dot_tn(vw_ref[pl.ds(start, WIN_SPAN), :], e.astype(BF16)) / jnp.maximum(l_win, TINY)

    sel_pen = jnp.transpose(jnp.where(sel_t > 0.5, 0.0, -MASK_BIG)).astype(BF16)
    return qs, o_cmp, o_win, sel_pen


def _nsa_kernel(q_ref, kc_ref, vc_ref, ks_ref, vs_ref, kw_ref, vw_ref, gp_ref, bg_ref,
                bias_ref, cov_ref, exp_ref, y_ref, acc_scr, *, nsel, topk, nd):
    step = pl.program_id(2)
    rq = NSA_REP * Q_BLOCK
    cols = NSA_QBLOCKS * rq
    blocks = [step * NSA_QBLOCKS + qb for qb in range(NSA_QBLOCKS)]
    fronts = [
        _nsa_front(i, q_ref[qb * Q_BLOCK:(qb + 1) * Q_BLOCK, :], kc_ref, vc_ref, kw_ref, vw_ref,
                   bias_ref, cov_ref, nsel=nsel, topk=topk, nd=nd)
        for qb, i in enumerate(blocks)]

    sub_tiles = SEL_TILE // Q_BLOCK
    q_aug = jnp.concatenate(
        [jnp.concatenate([qs, jnp.concatenate([sel_pen] * NSA_REP, axis=0)], axis=1)
         for qs, _, _, sel_pen in fronts], axis=0)
    acc_scr[...] = jnp.zeros(acc_scr.shape, F32)

    def sel_body(kt, carry):
        m_prev, l_prev = carry
        k0 = pl.multiple_of(kt * SEL_TILE, SEL_TILE)
        k_aug = jnp.concatenate([ks_ref[pl.ds(k0, SEL_TILE), :], exp_ref[kt]], axis=1)
        bias = jnp.concatenate(
            [jnp.concatenate(
                [_bias_tile(bias_ref, i - sub_tiles * kt - (KEY_TILE // Q_BLOCK) * u, nd) for i in blocks],
                axis=1)
             for u in range(SEL_TILE // KEY_TILE)], axis=0)
        sc = _dot_nt(k_aug, q_aug) + bias
        m_new = jnp.maximum(m_prev, jnp.max(sc, axis=0, keepdims=True))
        alpha = jnp.exp2(m_prev - m_new)
        e = jnp.exp2(sc - m_new)
        l_new = alpha * l_prev + jnp.sum(e, axis=0, keepdims=True)
        acc_scr[...] = alpha * acc_scr[...] + _dot_tn(vs_ref[pl.ds(k0, SEL_TILE), :], e.astype(BF16))
        return m_new, l_new

    init = (jnp.full((1, cols), NEG_INF, F32), jnp.zeros((1, cols), F32))
    _, l_sel = lax.fori_loop(0, blocks[-1] // sub_tiles + 1, sel_body, init)
    o_sel = acc_scr[...] / jnp.maximum(l_sel, TINY)
    for qb, (_, o_cmp, o_win, _) in enumerate(fronts):
        rows = slice(qb * Q_BLOCK, (qb + 1) * Q_BLOCK)
        _nsa_back(o_cmp, o_sel[:, qb * rq:(qb + 1) * rq], o_win, bg_ref[rows, :], gp_ref, y_ref, rows)


def _nsa_back(o_cmp, o_sel, o_win, branch_gates, gp_ref, y_ref, rows):
    dh = NSA_HEAD_DIM
    gates_t = jnp.transpose(_sigmoid(branch_gates.astype(F32)))

    def gate_row(branch):
        return jnp.concatenate(
            [gates_t[3 * r + branch:3 * r + branch + 1, :] for r in range(NSA_REP)], axis=1)

    o_t = gate_row(0) * o_cmp + gate_row(1) * o_sel + gate_row(2) * o_win
    for r in range(NSA_REP):
        o = jnp.transpose(o_t[:, r * Q_BLOCK:(r + 1) * Q_BLOCK])
        gp = gp_ref[rows, r * dh:(r + 1) * dh].astype(F32)
        y_ref[rows, r * dh:(r + 1) * dh] = (o * _silu(gp)).astype(BF16)


def _nsa_layout(s):
    nch = s // CMP_STRIDE
    nc = (s - CMP_BLOCK) // CMP_STRIDE + 1
    nsel = s // SEL_BLOCK
    start = np.arange(nc) * CMP_STRIDE
    sel_start = np.arange(nsel) * SEL_BLOCK
    cover = (start[:, None] < sel_start[None, :] + SEL_BLOCK) & (
        start[:, None] + CMP_BLOCK - 1 >= sel_start[None, :])
    cov_t = np.zeros((LANES, nch), np.float32)
    cov_t[:nsel, :nc] = cover.T
    key_blk = np.arange(s) // SEL_BLOCK
    expand = (key_blk[:, None] == np.arange(LANES)[None, :]).astype(np.float32)
    expand = expand.reshape(s // SEL_TILE, SEL_TILE, LANES)
    return jnp.asarray(cov_t, BF16), jnp.asarray(expand, BF16)


def _nsa_attention(proj_a, proj_b, cmp_kv, bias, bn, s):
    t = proj_a.shape[0]
    g_n, dh, rep = NSA_KV_GROUPS, NSA_HEAD_DIM, NSA_REP
    gw = rep * dh
    nq = s // Q_BLOCK
    nch = s // CMP_STRIDE
    nsel = s // SEL_BLOCK
    nd = bias.shape[0] - 1
    cov_t, expand = _nsa_layout(s)
    ksel0 = (NSA_WIDTH + 2 * NSA_KV_WIDTH) // dh
    vsel0 = 0
    kwin0 = NSA_KV_WIDTH // dh
    vwin0 = 2 * NSA_KV_WIDTH // dh
    kcol = lambda c0: pl.BlockSpec((s, dh), lambda b, g, i: (b, c0 + g))
    gp0 = 3 * NSA_KV_WIDTH // gw
    bg0 = (3 * NSA_KV_WIDTH + NSA_WIDTH) // LANES
    kern = functools.partial(_nsa_kernel, nsel=nsel, topk=min(SEL_TOPK, nsel), nd=nd)
    steps = nq // NSA_QBLOCKS
    rows = NSA_QBLOCKS * Q_BLOCK
    return pl.pallas_call(
        kern,
        grid=(bn, g_n, steps),
        in_specs=[
            pl.BlockSpec((rows, gw), lambda b, g, i: (b * steps + i, g)),
            pl.BlockSpec((None, nch, dh), lambda b, g, i: (0, b * g_n + g, 0)),
            pl.BlockSpec((None, nch, dh), lambda b, g, i: (1, b * g_n + g, 0)),
            kcol(ksel0), kcol(vsel0), kcol(kwin0), kcol(vwin0),
            pl.BlockSpec((rows, gw), lambda b, g, i: (b * steps + i, gp0 + g)),
            pl.BlockSpec((rows, LANES), lambda b, g, i: (b * steps + i, bg0 + g)),
            pl.BlockSpec((nd + 1, None, KEY_TILE, rep * Q_BLOCK), lambda b, g, i: (0, g, 0, 0),
                         pipeline_mode=pl.Buffered(1)),
            _resident(cov_t.shape, lambda b, g, i: (0, 0)),
            _resident(expand.shape, lambda b, g, i: (0, 0, 0)),
        ],
        out_specs=pl.BlockSpec((rows, gw), lambda b, g, i: (b * steps + i, g)),
        out_shape=jax.ShapeDtypeStruct((t, NSA_WIDTH), BF16),
        scratch_shapes=[pltpu.VMEM((dh, NSA_QBLOCKS * rep * Q_BLOCK), F32)],
        compiler_params=_params("parallel", "parallel", "arbitrary"),
        name="nsa_attention",
    )(proj_a, cmp_kv, cmp_kv, proj_a, proj_b, proj_b, proj_b, proj_b, proj_b, bias, cov_t, expand)


def _nsa_in_weights(w_in):
    split = NSA_WIDTH + 3 * NSA_KV_WIDTH
    main = NSA_WIDTH + 6 * NSA_KV_WIDTH
    ngate = 3 * NSA_HEADS
    per_group = ngate // NSA_KV_GROUPS
    d = w_in.shape[0]
    bg = w_in[:, main:main + ngate].reshape(d, NSA_KV_GROUPS, per_group)
    bg = jnp.pad(bg, ((0, 0), (0, 0), (0, LANES - per_group))).reshape(d, NSA_KV_GROUPS * LANES)
    w_b = jnp.concatenate([w_in[:, split:main], w_in[:, main + ngate:], bg], axis=1)
    return w_in[:, :split].astype(BF16), w_b.astype(BF16)


def _nsa_mixer(h, g_pre, bn, s, w_in, pos_k, w1_k, w2_k, pos_v, w1_v, w2_v, rel_bias):
    g_n, dh = NSA_KV_GROUPS, NSA_HEAD_DIM
    w_a, w_b = _nsa_in_weights(w_in)
    proj_a = _norm_matmul(h, g_pre, w_a)
    proj_b = _norm_matmul(h, g_pre, w_b)
    nch = s // CMP_STRIDE
    p3 = proj_a.reshape(bn, nch, CMP_STRIDE, proj_a.shape[1])

    def chunks(c0):
        x = p3[..., c0:c0 + NSA_KV_WIDTH].reshape(bn, nch, CMP_STRIDE, g_n, dh)
        return jnp.transpose(x, (0, 3, 1, 2, 4)).reshape(bn * g_n * nch, CMP_STRIDE * dh)

    xkv = jnp.stack([chunks(NSA_WIDTH), chunks(NSA_WIDTH + NSA_KV_WIDTH)])
    half = CMP_STRIDE * dh
    pos = jnp.stack([pos_k.reshape(2, half), pos_v.reshape(2, half)])
    w1 = jnp.stack([w1_k.reshape(2, half, -1), w1_v.reshape(2, half, -1)]).astype(BF16)
    w2 = jnp.stack([w2_k, w2_v]).astype(BF16)
    cmp_kv = _compress(xkv, pos, w1, w2, nch)
    nd = min(s // Q_BLOCK, 15)
    bias = _causal_bias(rel_bias, nd)
    return _nsa_attention(proj_a, proj_b, cmp_kv, bias, bn, s)


def kernel(x, p, rel_bias, norm_pre, norm_post, ple_w_proj, ple_w_gate, a_w_in, a_conv_w, a_conv_b,
           a_w_r, a_b_r, a_w_i, a_b_i, a_lam, a_w_out, b_w_in, b_w_out, c_w_in, c_cmp_pos_k,
           c_cmp_w1_k, c_cmp_w2_k, c_cmp_pos_v, c_cmp_w1_v, c_cmp_w2_v, c_w_out):
    bn, s, d = x.shape
    depth = p.shape[0]
    t = bn * s
    h = x.reshape(t, d)
    for i in range(depth):
        kind = i % N_MIXERS
        j = i // N_MIXERS
        if kind == 0:
            proj = _norm_matmul(h, norm_pre[i], a_w_in[j].astype(BF16))
            y = _rglru(proj, bn, s, a_conv_w[j], a_conv_b[j], a_w_r[j], a_b_r[j], a_w_i[j],
                       a_b_i[j], a_lam[j])
            w_out = a_w_out[j]
        elif kind == 1:
            y = _dilated_mixer(h, norm_pre[i], bn, s, b_w_in[j], rel_bias)
            w_out = b_w_out[j]
        else:
            y = _nsa_mixer(h, norm_pre[i], bn, s, c_w_in[j], c_cmp_pos_k[j], c_cmp_w1_k[j],
                           c_cmp_w2_k[j], c_cmp_pos_v[j], c_cmp_w1_v[j], c_cmp_w2_v[j], rel_bias)
            w_out = c_w_out[j]
        h = _out_ple(y, w_out.astype(BF16), norm_post[i], h, p[i].reshape(t, -1),
                     ple_w_proj[i].astype(BF16), ple_w_gate[i].astype(BF16))
    return h.reshape(bn, s, d)
```

```python
import functools
import math

import numpy as np
import jax
import jax.numpy as jnp
from jax import lax
from jax.experimental import pallas as pl
from jax.experimental.pallas import tpu as pltpu

F32 = jnp.float32
BF16 = jnp.bfloat16

NORM_EPS = 1e-6
NEG_INF = -1e30
TINY = 1e-30
MASK_BIG = 2.0 ** 40
LOG2_E = math.log2(math.e)

N_MIXERS = 3
N_BUCKETS = 32
BUCKET_MAX_DIST = 2048

LRU_BLOCKS = 8
CONV_W = 4
LRU_C = 8.0

DIL_PATTERNS = ((128, 1), (512, 4), (2048, 16))
DIL_HEADS = 16
DIL_HEAD_DIM = 64
DIL_WIDTH = DIL_HEADS * DIL_HEAD_DIM

NSA_HEADS = 16
NSA_KV_GROUPS = 4
NSA_REP = NSA_HEADS // NSA_KV_GROUPS
NSA_HEAD_DIM = 128
NSA_WIDTH = NSA_HEADS * NSA_HEAD_DIM
NSA_KV_WIDTH = NSA_KV_GROUPS * NSA_HEAD_DIM
CMP_BLOCK = 32
CMP_STRIDE = 16
SEL_BLOCK = 64
SEL_TOPK = 16
WIN = 512
FORCE_BONUS = 1e4

SUBLANES = 8
LANES = 128
Q_BLOCK = 128
KEY_TILE = 256
DIL_WINDOW = Q_BLOCK * max(d for _, d in DIL_PATTERNS)
SEL_TILE = 1024
NSA_QBLOCKS = 4
PROJ_ROWS = 512
PROJ_HALVES = 2
RESIDUE_STAGE = 4
WIN_SPAN = WIN + Q_BLOCK
VMEM_LIMIT_BYTES = 56 * 1024 * 1024


def _params(*sem):
    return pltpu.CompilerParams(dimension_semantics=sem, vmem_limit_bytes=VMEM_LIMIT_BYTES)


def _dot(a, b):
    return jnp.dot(a, b, preferred_element_type=F32)


def _dot_nt(a, b):
    return lax.dot_general(a, b, (((1,), (1,)), ((), ())), preferred_element_type=F32)


def _dot_tn(a, b):
    return lax.dot_general(a, b, (((0,), (0,)), ((), ())), preferred_element_type=F32)


def _sigmoid(x):
    return 1.0 / (1.0 + jnp.exp(-x))


def _silu(x):
    return x * _sigmoid(x)


def _resident(block_shape, index_map):
    return pl.BlockSpec(block_shape, index_map, pipeline_mode=pl.Buffered(1))


def _norm_matmul_kernel(h_ref, g_ref, w_ref, o_ref, *scratch, dil):
    n = w_ref.shape[1]
    half_rows = h_ref.shape[0] // PROJ_HALVES
    for hh in range(PROJ_HALVES):
        x = h_ref[hh * half_rows:(hh + 1) * half_rows, :]
        ms = jnp.mean(x * x, axis=-1, keepdims=True)
        u = (x * lax.rsqrt(ms + NORM_EPS) * g_ref[...]).astype(BF16)
        res = _dot(u, w_ref[...])
        if dil == 1:
            o_ref[hh * half_rows:(hh + 1) * half_rows, :] = res.astype(o_ref.dtype)
            continue
        res_scr = scratch[0]
        per = half_rows // dil
        for c in range(n // LANES):
            res_scr[hh, c] = res[:, c * LANES:(c + 1) * LANES]
        if dil <= RESIDUE_STAGE:
            for r in range(dil):
                for c in range(n // LANES):
                    o_ref[r, hh * per:(hh + 1) * per, c * LANES:(c + 1) * LANES] = (
                        res_scr[hh, c, pl.ds(r, per, stride=dil), :].astype(o_ref.dtype))
            continue
        mid_scr = scratch[1]
        outer = dil // RESIDUE_STAGE
        cls_rows = half_rows // RESIDUE_STAGE
        for c in range(n // LANES):
            for r1 in range(RESIDUE_STAGE):
                mid_scr[hh, c, r1 * cls_rows:(r1 + 1) * cls_rows, :] = (
                    res_scr[hh, c, pl.ds(r1, cls_rows, stride=RESIDUE_STAGE), :])
        for r1 in range(RESIDUE_STAGE):
            for r2 in range(outer):
                for c in range(n // LANES):
                    o_ref[r1 + RESIDUE_STAGE * r2, hh * per:(hh + 1) * per, c * LANES:(c + 1) * LANES] = (
                        mid_scr[hh, c, pl.ds(r1 * cls_rows + r2, per, stride=outer), :].astype(o_ref.dtype))


def _norm_matmul(h, g, w, dil=1, bn=1):
    t, d = h.shape
    n = w.shape[1]
    tm = min(t, PROJ_ROWS)
    in_specs = [
        pl.BlockSpec((tm, d), lambda i: (i, 0)),
        _resident((1, d), lambda i: (0, 0)),
        _resident((d, n), lambda i: (0, 0)),
    ]
    if dil > 1:
        s = t // bn
        nt = s // tm
        out_spec = pl.BlockSpec((None, dil, tm // dil, n), lambda i: (i // nt, 0, i % nt, 0))
        out_shape = jax.ShapeDtypeStruct((bn, dil, s // dil, n), BF16)
        slabs = pltpu.VMEM((PROJ_HALVES, n // LANES, tm // PROJ_HALVES, LANES), F32)
        scratch = [slabs] if dil <= RESIDUE_STAGE else [slabs, slabs]
    else:
        out_spec = pl.BlockSpec((tm, n), lambda i: (i, 0))
        out_shape = jax.ShapeDtypeStruct((t, n), BF16)
        scratch = []
    return pl.pallas_call(
        functools.partial(_norm_matmul_kernel, dil=dil),
        grid=(t // tm,),
        in_specs=in_specs,
        out_specs=out_spec,
        out_shape=out_shape,
        scratch_shapes=scratch,
        compiler_params=_params("parallel"),
        name="norm_matmul" if dil == 1 else f"norm_matmul_residue_{dil}",
    )(h, g.reshape(1, d), w)


def _out_ple_kernel(y_ref, wo_ref, g_ref, h_ref, p_ref, wp_ref, wg_ref, o_ref):
    z = _dot(y_ref[...], wo_ref[...])
    ms = jnp.mean(z * z, axis=-1, keepdims=True)
    h1 = h_ref[...] + z * lax.rsqrt(ms + NORM_EPS) * g_ref[...]
    gate = _sigmoid(_dot(h1.astype(BF16), wg_ref[...]))
    pe = _dot(p_ref[...].astype(BF16), wp_ref[...])
    o_ref[...] = h1 + gate * pe


def _out_ple(y, w_out, g_post, h, p, w_proj, w_gate):
    t, d = h.shape
    k = y.shape[1]
    pd = p.shape[1]
    tm = min(t, 256)
    return pl.pallas_call(
        _out_ple_kernel,
        grid=(t // tm,),
        in_specs=[
            pl.BlockSpec((tm, k), lambda i: (i, 0)),
            _resident((k, d), lambda i: (0, 0)),
            _resident((1, d), lambda i: (0, 0)),
            pl.BlockSpec((tm, d), lambda i: (i, 0)),
            pl.BlockSpec((tm, pd), lambda i: (i, 0)),
            _resident((pd, d), lambda i: (0, 0)),
            _resident((d, d), lambda i: (0, 0)),
        ],
        out_specs=pl.BlockSpec((tm, d), lambda i: (i, 0)),
        out_shape=jax.ShapeDtypeStruct((t, d), F32),
        compiler_params=_params("parallel"),
        name="out_ple",
    )(y, w_out, g_post.reshape(1, d), h, p, w_proj, w_gate)


def _lru_kernel(xb_ref, gate_ref, cw_ref, cb_ref, wr_ref, br_ref, wi_ref, bi_ref, lam_ref,
                y_ref, xext_scr, h_scr, a_scr, b_scr, h_scan, *, ts, bw):
    @pl.when(pl.program_id(1) == 0)
    def _():
        xext_scr[0:SUBLANES, :] = jnp.zeros((SUBLANES, xext_scr.shape[1]), F32)
        h_scr[...] = jnp.zeros(h_scr.shape, F32)

    seg = ts // SUBLANES
    pitch = seg + SUBLANES
    row = lax.broadcasted_iota(jnp.int32, (SUBLANES, LANES), 0)
    for n in range(LRU_BLOCKS):
        cols = slice(n * bw, (n + 1) * bw)
        x = xb_ref[:, cols].astype(F32)
        xext_scr[SUBLANES:SUBLANES + ts, cols] = x
        xc = cw_ref[3:4, cols] * x + cb_ref[:, cols]
        for k in range(1, CONV_W):
            xc = xc + cw_ref[3 - k:4 - k, cols] * xext_scr[SUBLANES - k:SUBLANES - k + ts, cols]
        xext_scr[0:SUBLANES, cols] = x[ts - SUBLANES:ts, :]

        xn = xc.astype(BF16)
        r = _sigmoid(_dot(xn, wr_ref[n]) + br_ref[:, cols])
        ig = _sigmoid(_dot(xn, wi_ref[n]) + bi_ref[:, cols])
        nl = -lam_ref[:, cols]
        softplus = jnp.maximum(nl, 0.0) + jnp.log(1.0 + jnp.exp(-jnp.abs(nl)))
        a = jnp.exp2(r * ((-LRU_C * LOG2_E) * softplus))
        b = jnp.sqrt(1.0 - a * a) * (ig * xc)

        for sl in range(bw // LANES):
            lanes = slice(n * bw + sl * LANES, n * bw + (sl + 1) * LANES)
            slab = n * (bw // LANES) + sl
            for sg in range(SUBLANES):
                a_scr[slab, sg * pitch:sg * pitch + seg, :] = a[sg * seg:(sg + 1) * seg, sl * LANES:(sl + 1) * LANES]
                b_scr[slab, sg * pitch:sg * pitch + seg, :] = b[sg * seg:(sg + 1) * seg, sl * LANES:(sl + 1) * LANES]
            h_loc = jnp.zeros((SUBLANES, LANES), F32)
            a_loc = jnp.ones((SUBLANES, LANES), F32)
            h_steps, a_steps = [], []
            for j in range(seg):
                a_j = a_scr[slab, pl.ds(j, SUBLANES, stride=pitch), :]
                b_j = b_scr[slab, pl.ds(j, SUBLANES, stride=pitch), :]
                h_loc = a_j * h_loc + b_j
                a_loc = a_j * a_loc
                h_steps.append(h_loc)
                a_steps.append(a_loc)
            a_tot, h_tot = a_loc, h_loc
            for dist in (1, 2, 4):
                a_sh = jnp.where(row >= dist, pltpu.roll(a_tot, dist, axis=0), 1.0)
                h_sh = jnp.where(row >= dist, pltpu.roll(h_tot, dist, axis=0), 0.0)
                h_tot = a_tot * h_sh + h_tot
                a_tot = a_tot * a_sh
            h_in = h_scr[:, lanes]
            h_end = a_tot * h_in + h_tot
            h_enter = jnp.where(row >= 1, pltpu.roll(h_end, 1, axis=0), h_in)
            for j in range(seg):
                h_scan[slab, pl.ds(j, SUBLANES, stride=pitch), :] = h_steps[j] + a_steps[j] * h_enter
            h_scr[:, lanes] = jnp.broadcast_to(h_end[SUBLANES - 1:SUBLANES, :], (SUBLANES, LANES))
            hs = jnp.concatenate(
                [h_scan[slab, sg * pitch:sg * pitch + seg, :] for sg in range(SUBLANES)], axis=0)
            y_ref[:, lanes] = (hs * _silu(gate_ref[:, lanes].astype(F32))).astype(BF16)


def _rglru(proj, bn, s, conv_w, conv_b, w_r, b_r, w_i, b_i, lam):
    t, two_c = proj.shape
    c = two_c // 2
    bw = c // LRU_BLOCKS
    ts = min(s, 256)
    nst = s // ts
    row = lambda v: v.reshape(1, c)
    kern = functools.partial(_lru_kernel, ts=ts, bw=bw)
    return pl.pallas_call(
        kern,
        grid=(bn, nst),
        in_specs=[
            pl.BlockSpec((ts, c), lambda b, i: (b * nst + i, 0)),
            pl.BlockSpec((ts, c), lambda b, i: (b * nst + i, 1)),
            pl.BlockSpec((CONV_W, c), lambda b, i: (0, 0)),
            pl.BlockSpec((1, c), lambda b, i: (0, 0)),
            pl.BlockSpec((LRU_BLOCKS, bw, bw), lambda b, i: (0, 0, 0)),
            pl.BlockSpec((1, c), lambda b, i: (0, 0)),
            pl.BlockSpec((LRU_BLOCKS, bw, bw), lambda b, i: (0, 0, 0)),
            pl.BlockSpec((1, c), lambda b, i: (0, 0)),
            pl.BlockSpec((1, c), lambda b, i: (0, 0)),
        ],
        out_specs=pl.BlockSpec((ts, c), lambda b, i: (b * nst + i, 0)),
        out_shape=jax.ShapeDtypeStruct((t, c), BF16),
        scratch_shapes=[pltpu.VMEM((ts + SUBLANES, c), F32), pltpu.VMEM((SUBLANES, c), F32)]
        + [pltpu.VMEM((c // LANES, ts + SUBLANES * SUBLANES, LANES), F32)] * 3,
        compiler_params=_params("parallel", "arbitrary"),
        name="rglru",
    )(proj, proj, conv_w, row(conv_b), w_r.astype(BF16), row(b_r), w_i.astype(BF16), row(b_i),
      row(lam))


def _t5_bucket_np(dist):
    n = np.maximum(dist, 0)
    max_exact = N_BUCKETS // 2
    nf = np.maximum(n, max_exact).astype(np.float64)
    large = max_exact + (np.log(nf / max_exact) / math.log(BUCKET_MAX_DIST / max_exact)
                         * (N_BUCKETS - max_exact)).astype(np.int64)
    return np.where(n < max_exact, n, np.minimum(large, N_BUCKETS - 1)).astype(np.int32)


TOEPLITZ_PERIOD = Q_BLOCK + KEY_TILE


def _toeplitz_kernel(w_ref, o_ref, *, rows, cols, side_by_side, scale):
    for h in range(w_ref.shape[0]):
        gen = jnp.broadcast_to(w_ref[h:h + 1, :] * scale, (rows, w_ref.shape[1]))
        tile = pltpu.roll(gen, 0, 1, stride=1, stride_axis=0)[:, :cols]
        o_ref[h // side_by_side, :, (h % side_by_side) * cols:(h % side_by_side + 1) * cols] = tile


def _toeplitz(w, rows, cols, side_by_side=1, scale=1.0):
    n, heads, period = w.shape
    out_block = (None, heads // side_by_side, rows, side_by_side * cols)
    return pl.pallas_call(
        functools.partial(_toeplitz_kernel, rows=rows, cols=cols, side_by_side=side_by_side, scale=scale),
        grid=(n,),
        in_specs=[pl.BlockSpec((None, heads, period), lambda i: (i, 0, 0))],
        out_specs=pl.BlockSpec(out_block, lambda i: (i, 0, 0, 0)),
        out_shape=jax.ShapeDtypeStruct((n,) + out_block[1:], F32),
        compiler_params=_params("parallel"),
        name="toeplitz_bias",
    )(w)


def _signed_offsets(n_cols):
    m = np.arange(TOEPLITZ_PERIOD)
    return np.where(m < n_cols, m, m - TOEPLITZ_PERIOD)


def _dilated_bias(rel_bias):
    delta = Q_BLOCK - _signed_offsets(2 * Q_BLOCK)
    gens = []
    for window, dil in DIL_PATTERNS:
        ok = (delta >= 0) & (delta <= window // dil)
        gens.append(jnp.where(jnp.asarray(ok)[:, None], rel_bias[_t5_bucket_np(delta * dil)], NEG_INF))
    tbl = _toeplitz(jnp.transpose(jnp.stack(gens), (0, 2, 1)).astype(F32), Q_BLOCK, 2 * Q_BLOCK,
                    scale=LOG2_E)
    return [tbl[g] for g in range(len(DIL_PATTERNS))]


def _causal_bias(rel_bias, n_dist):
    heads = rel_bias.shape[1]
    dist = np.arange(n_dist)[:, None] * Q_BLOCK + _signed_offsets(Q_BLOCK)[None, :]
    vec = jnp.where(jnp.asarray(dist >= 0)[..., None], rel_bias[_t5_bucket_np(dist)], NEG_INF)
    vec = jnp.transpose(vec, (0, 2, 1)).astype(F32)
    vec = jnp.concatenate([vec, jnp.full((1, heads, TOEPLITZ_PERIOD), NEG_INF, F32)])
    return _toeplitz(vec, KEY_TILE, Q_BLOCK, side_by_side=NSA_REP, scale=LOG2_E)


def _dil_kernel(*refs, dils):
    ng = len(dils)
    groups = [refs[5 * g:5 * g + 5] for g in range(ng)]
    bias_refs = refs[5 * ng:6 * ng]
    gate_ref, y_ref = refs[6 * ng], refs[6 * ng + 1]
    ext = refs[6 * ng + 2:8 * ng + 2]
    o_acc, lse_acc = refs[8 * ng + 2:]
    first = pl.program_id(1) == 0
    scale = DIL_HEAD_DIM ** -0.5 * LOG2_E
    lane = lax.broadcasted_iota(jnp.int32, (Q_BLOCK, LANES), 1)
    low = lane < DIL_HEAD_DIM
    col = lax.broadcasted_iota(jnp.int32, (Q_BLOCK, 2 * Q_BLOCK), 1)
    halo_pen = jnp.where(col < Q_BLOCK, jnp.where(first, NEG_INF, 0.0), 0.0)
    for g, dil in enumerate(dils):
        q_ref, k_ref, kh_ref, v_ref, vh_ref = groups[g]
        kext, vext = ext[2 * g], ext[2 * g + 1]
        kext[:, 0:Q_BLOCK, :] = kh_ref[...]
        kext[:, Q_BLOCK:, :] = k_ref[...]
        vext[:, 0:Q_BLOCK, :] = vh_ref[...]
        vext[:, Q_BLOCK:, :] = v_ref[...]
        for r in range(dil):
            for j in range(k_ref.shape[1] // Q_BLOCK):
                qp = q_ref[r, j * Q_BLOCK:(j + 1) * Q_BLOCK, :]
                kb = kext[r, j * Q_BLOCK:(j + 2) * Q_BLOCK, :]
                vb = vext[r, j * Q_BLOCK:(j + 2) * Q_BLOCK, :]
                o_half, lse_half = [], []
                for half in range(2):
                    qm = jnp.where(low if half == 0 else jnp.logical_not(low), qp, jnp.zeros_like(qp))
                    sc = _dot_nt(qm, kb) * scale + bias_refs[g][half]
                    if j == 0:
                        sc = sc + halo_pen
                    m = jnp.max(sc, axis=-1, keepdims=True)
                    e = jnp.exp2(sc - m)
                    l = jnp.sum(e, axis=-1, keepdims=True)
                    pv = _dot(e.astype(BF16), vb)
                    o_half.append(pv / l)
                    lse_half.append(jnp.broadcast_to(m + jnp.log2(l), (Q_BLOCK, LANES)))
                o_new = jnp.where(low, o_half[0], o_half[1])
                lse_new = jnp.where(low, lse_half[0], lse_half[1])
                if dil == 1:
                    rows = pl.ds(j * Q_BLOCK, Q_BLOCK)
                else:
                    rows = pl.ds(r + dil * j * Q_BLOCK, Q_BLOCK, stride=dil)
                if g == 0:
                    o_acc[rows, :] = o_new
                    lse_acc[rows, :] = lse_new
                else:
                    o_old = o_acc[rows, :]
                    lse_old = lse_acc[rows, :]
                    top = jnp.maximum(lse_old, lse_new)
                    w_old = jnp.exp2(lse_old - top)
                    w_new = jnp.exp2(lse_new - top)
                    tot = w_old + w_new
                    o_acc[rows, :] = (w_old * o_old + w_new * o_new) / tot
                    if g + 1 < ng:
                        lse_acc[rows, :] = top + jnp.log2(tot)
    y_ref[...] = (o_acc[...] * _silu(gate_ref[...].astype(F32))).astype(BF16)


def _dil_attention(arrays, biases, bn, s):
    dils = tuple(d for _, d in DIL_PATTERNS)
    win = DIL_WINDOW
    pairs = DIL_WIDTH // LANES
    nw = s // win
    in_specs, operands, scratch = [], [], []
    for g, dil in enumerate(dils):
        per = win // dil
        per_blocks = per // Q_BLOCK
        cur = lambda cb, dil=dil, per=per: pl.BlockSpec(
            (None, dil, per, LANES), lambda b, w, p: (b, 0, w, cb * pairs + p))
        halo = lambda cb, dil=dil, per_blocks=per_blocks: pl.BlockSpec(
            (None, dil, Q_BLOCK, LANES),
            lambda b, w, p: (b, 0, jnp.maximum(w * per_blocks - 1, 0), cb * pairs + p))
        in_specs += [cur(0), cur(1), halo(1), cur(2), halo(2)]
        operands += [arrays[g]] * 5
        scratch += [pltpu.VMEM((dil, per + Q_BLOCK, LANES), BF16)] * 2
    in_specs += [pl.BlockSpec((2, Q_BLOCK, 2 * Q_BLOCK), lambda b, w, p: (p, 0, 0))] * len(dils)
    operands += list(biases)
    in_specs.append(pl.BlockSpec((None, None, win, LANES), lambda b, w, p: (b, 0, w, 3 * pairs + p)))
    operands.append(arrays[0])
    scratch += [pltpu.VMEM((win, LANES), F32)] * 2
    return pl.pallas_call(
        functools.partial(_dil_kernel, dils=dils),
        grid=(bn, nw, pairs),
        in_specs=in_specs,
        out_specs=pl.BlockSpec((win, LANES), lambda b, w, p: (b * nw + w, p)),
        out_shape=jax.ShapeDtypeStruct((bn * s, DIL_WIDTH), BF16),
        scratch_shapes=scratch,
        compiler_params=_params("parallel", "arbitrary", "arbitrary"),
        name="dilated_attention",
    )(*operands)


def _dilated_mixer(h, g_pre, bn, s, w_in, rel_bias):
    w = DIL_WIDTH
    ng = len(DIL_PATTERNS)
    wb = w_in.astype(BF16)
    main_w = jnp.concatenate([wb[:, :3 * w], wb[:, 3 * ng * w:]], axis=1)
    arrays = [_norm_matmul(h, g_pre, main_w).reshape(bn, 1, s, 4 * w)]
    for gi in range(1, ng):
        dil = DIL_PATTERNS[gi][1]
        arrays.append(_norm_matmul(h, g_pre, wb[:, 3 * gi * w:3 * (gi + 1) * w], dil=dil, bn=bn))
    return _dil_attention(arrays, _dilated_bias(rel_bias), bn, s)


def _gelu_tanh(x):
    return 0.5 * x * (1.0 + jnp.tanh(math.sqrt(2.0 / math.pi) * (x + 0.044715 * (x * x * x))))


def _compress_kernel(x_ref, pos_ref, w1_ref, w2_ref, o_ref, *, nch):
    x = x_ref[...].astype(F32)
    top = _dot((x + pos_ref[0:1, :]).astype(BF16), w1_ref[0])
    bot = _dot((x + pos_ref[1:2, :]).astype(BF16), w1_ref[1])
    hid = top + pltpu.roll(bot, nch - 1, axis=0)
    o_ref[...] = _dot(_gelu_tanh(hid).astype(BF16), w2_ref[...]).astype(BF16)


def _compress(xkv, pos, w1, w2, nch):
    _, rows, width = xkv.shape
    hidden = w1.shape[-1]
    dh = w2.shape[-1]
    return pl.pallas_call(
        functools.partial(_compress_kernel, nch=nch),
        grid=(2, rows // nch),
        in_specs=[
            pl.BlockSpec((None, nch, width), lambda kv, i: (kv, i, 0)),
            pl.BlockSpec((None, 2, width), lambda kv, i: (kv, 0, 0)),
            pl.BlockSpec((None, 2, width, hidden), lambda kv, i: (kv, 0, 0, 0)),
            pl.BlockSpec((None, hidden, dh), lambda kv, i: (kv, 0, 0)),
        ],
        out_specs=pl.BlockSpec((None, nch, dh), lambda kv, i: (kv, i, 0)),
        out_shape=jax.ShapeDtypeStruct((2, rows, dh), BF16),
        compiler_params=_params("parallel", "parallel"),
        name="nsa_compress",
    )(xkv, pos, w1, w2)


def _bias_tile(bias_ref, d, nd):
    return bias_ref[jnp.where(d < 0, nd, jnp.minimum(d, nd - 1))]


def _nsa_front(i, q_blk, kc_ref, vc_ref, kw_ref, vw_ref, bias_ref, cov_ref, *, nsel, topk, nd):
    t0 = i * Q_BLOCK
    dh = NSA_HEAD_DIM
    rq = NSA_REP * Q_BLOCK
    scale = dh ** -0.5 * LOG2_E
    q_all = jnp.concatenate([q_blk[:, r * dh:(r + 1) * dh] for r in range(NSA_REP)], axis=0)
    qs = (q_all.astype(F32) * scale).astype(BF16)

    def bias_tile(d):
        return _bias_tile(bias_ref, d, nd)

    def per_head(x):
        return jnp.concatenate([x] * NSA_REP, axis=1)

    kc = kc_ref[...]
    nch = kc.shape[0]
    t_q = t0 + (lax.broadcasted_iota(jnp.int32, (nch, rq), 1) & (Q_BLOCK - 1))
    cend = lax.broadcasted_iota(jnp.int32, (nch, rq), 0) * CMP_STRIDE + (CMP_BLOCK - 1)
    cmask = cend <= t_q
    sc = jnp.where(cmask, _dot_nt(kc, qs), NEG_INF)
    m = jnp.max(sc, axis=0, keepdims=True)
    e = jnp.where(cmask, jnp.exp2(sc - m), 0.0)
    prob = e / jnp.maximum(jnp.sum(e, axis=0, keepdims=True), TINY)
    o_cmp = _dot_tn(vc_ref[...], prob.astype(BF16))
    psum = prob[:, 0:Q_BLOCK]
    for r in range(1, NSA_REP):
        psum = psum + prob[:, r * Q_BLOCK:(r + 1) * Q_BLOCK]
    p_hi = psum.astype(BF16)
    p_lo = (psum - p_hi.astype(F32)).astype(BF16)
    cov = cov_ref[...]
    imp = (_dot(cov, p_hi) + _dot(cov, p_lo))[0:nsel, :]
    blk = lax.broadcasted_iota(jnp.int32, (nsel, Q_BLOCK), 0)
    t_col = t0 + lax.broadcasted_iota(jnp.int32, (nsel, Q_BLOCK), 1)
    cur = t_col // SEL_BLOCK
    forced = (blk == 0) | (blk == cur) | (blk == cur - 1)
    valid = blk * SEL_BLOCK <= t_col
    imp = jnp.where(valid, imp + jnp.where(forced, FORCE_BONUS, 0.0), NEG_INF)
    ngrp = nsel // SUBLANES
    rank = [jnp.zeros((SUBLANES, Q_BLOCK), jnp.int32) for _ in range(ngrp)]
    parts = [imp[g * SUBLANES:(g + 1) * SUBLANES, :] for g in range(ngrp)]
    sub = lax.broadcasted_iota(jnp.int32, (SUBLANES, Q_BLOCK), 0)
    for mrow in range(nsel):
        other = jnp.broadcast_to(imp[mrow:mrow + 1, :], (SUBLANES, Q_BLOCK))
        for g in range(ngrp):
            ge = jnp.where(other >= parts[g], 1, 0)
            gt = jnp.where(other > parts[g], 1, 0)
            if g * SUBLANES > mrow:
                before = ge
            elif (g + 1) * SUBLANES - 1 < mrow:
                before = gt
            else:
                before = jnp.where(sub + g * SUBLANES > mrow, ge, gt)
            rank[g] = rank[g] + before
    rank = jnp.concatenate(rank, axis=0)
    sel_t = jnp.where(valid & (rank < topk), 1.0, 0.0)
    if nsel < LANES:
        sel_t = jnp.concatenate([sel_t, jnp.zeros((LANES - nsel, Q_BLOCK), F32)], axis=0)

    j0 = jnp.maximum(i - WIN // Q_BLOCK, 0)
    start = pl.multiple_of(j0 * Q_BLOCK, Q_BLOCK)
    dbase = i - j0
    sc = _dot_nt(kw_ref[pl.ds(start, WIN_SPAN), :], qs)
    dist = (dbase * Q_BLOCK + lax.broadcasted_iota(jnp.int32, (Q_BLOCK, Q_BLOCK), 1)
            - lax.broadcasted_iota(jnp.int32, (Q_BLOCK, Q_BLOCK), 0))
    too_old = per_head(jnp.where(dist <= WIN - 1, 0.0, NEG_INF))
    n_full, rem = divmod(WIN_SPAN, KEY_TILE)
    pieces = [bias_tile(dbase - (KEY_TILE // Q_BLOCK) * u) for u in range(n_full)]
    if rem:
        pieces.append(bias_tile(dbase - (KEY_TILE // Q_BLOCK) * n_full)[0:rem])
    pieces[0] = jnp.concatenate([pieces[0][0:Q_BLOCK] + too_old, pieces[0][Q_BLOCK:]], axis=0)
    sc = sc + jnp.concatenate(pieces, axis=0)
    m = jnp.max(sc, axis=0, keepdims=True)
    e = jnp.exp2(sc - m)
    l_win = jnp.sum(e, axis=0, keepdims=True)
    o_win = _dot_tn(vw_ref[pl.ds(start, WIN_SPAN), :], e.astype(BF16)) / jnp.maximum(l_win, TINY)

    sel_pen = jnp.transpose(jnp.where(sel_t > 0.5, 0.0, -MASK_BIG)).astype(BF16)
    return qs, o_cmp, o_win, sel_pen


def _nsa_kernel(q_ref, kc_ref, vc_ref, ks_ref, vs_ref, kw_ref, vw_ref, gp_ref, bg_ref,
                bias_ref, cov_ref, exp_ref, y_ref, acc_scr, *, nsel, topk, nd):
    step = pl.program_id(2)
    rq = NSA_REP * Q_BLOCK
    cols = NSA_QBLOCKS * rq
    blocks = [step * NSA_QBLOCKS + qb for qb in range(NSA_QBLOCKS)]
    fronts = [
        _nsa_front(i, q_ref[qb * Q_BLOCK:(qb + 1) * Q_BLOCK, :], kc_ref, vc_ref, kw_ref, vw_ref,
                   bias_ref, cov_ref, nsel=nsel, topk=topk, nd=nd)
        for qb, i in enumerate(blocks)]

    sub_tiles = SEL_TILE // Q_BLOCK
    q_aug = jnp.concatenate(
        [jnp.concatenate([qs, jnp.concatenate([sel_pen] * NSA_REP, axis=0)], axis=1)
         for qs, _, _, sel_pen in fronts], axis=0)
    acc_scr[...] = jnp.zeros(acc_scr.shape, F32)

    def sel_body(kt, carry):
        m_prev, l_prev = carry
        k0 = pl.multiple_of(kt * SEL_TILE, SEL_TILE)
        k_aug = jnp.concatenate([ks_ref[pl.ds(k0, SEL_TILE), :], exp_ref[kt]], axis=1)
        bias = jnp.concatenate(
            [jnp.concatenate(
                [_bias_tile(bias_ref, i - sub_tiles * kt - (KEY_TILE // Q_BLOCK) * u, nd) for i in blocks],
                axis=1)
             for u in range(SEL_TILE // KEY_TILE)], axis=0)
        sc = _dot_nt(k_aug, q_aug) + bias
        m_new = jnp.maximum(m_prev, jnp.max(sc, axis=0, keepdims=True))
        alpha = jnp.exp2(m_prev - m_new)
        e = jnp.exp2(sc - m_new)
        l_new = alpha * l_prev + jnp.sum(e, axis=0, keepdims=True)
        acc_scr[...] = alpha * acc_scr[...] + _dot_tn(vs_ref[pl.ds(k0, SEL_TILE), :], e.astype(BF16))
        return m_new, l_new

    init = (jnp.full((1, cols), NEG_INF, F32), jnp.zeros((1, cols), F32))
    _, l_sel = lax.fori_loop(0, blocks[-1] // sub_tiles + 1, sel_body, init)
    o_sel = acc_scr[...] / jnp.maximum(l_sel, TINY)
    for qb, (_, o_cmp, o_win, _) in enumerate(fronts):
        rows = slice(qb * Q_BLOCK, (qb + 1) * Q_BLOCK)
        _nsa_back(o_cmp, o_sel[:, qb * rq:(qb + 1) * rq], o_win, bg_ref[rows, :], gp_ref, y_ref, rows)


def _nsa_back(o_cmp, o_sel, o_win, branch_gates, gp_ref, y_ref, rows):
    dh = NSA_HEAD_DIM
    gates_t = jnp.transpose(_sigmoid(branch_gates.astype(F32)))

    def gate_row(branch):
        return jnp.concatenate(
            [gates_t[3 * r + branch:3 * r + branch + 1, :] for r in range(NSA_REP)], axis=1)

    o_t = gate_row(0) * o_cmp + gate_row(1) * o_sel + gate_row(2) * o_win
    for r in range(NSA_REP):
        o = jnp.transpose(o_t[:, r * Q_BLOCK:(r + 1) * Q_BLOCK])
        gp = gp_ref[rows, r * dh:(r + 1) * dh].astype(F32)
        y_ref[rows, r * dh:(r + 1) * dh] = (o * _silu(gp)).astype(BF16)


def _nsa_layout(s):
    nch = s // CMP_STRIDE
    nc = (s - CMP_BLOCK) // CMP_STRIDE + 1
    nsel = s // SEL_BLOCK
    start = np.arange(nc) * CMP_STRIDE
    sel_start = np.arange(nsel) * SEL_BLOCK
    cover = (start[:, None] < sel_start[None, :] + SEL_BLOCK) & (
        start[:, None] + CMP_BLOCK - 1 >= sel_start[None, :])
    cov_t = np.zeros((LANES, nch), np.float32)
    cov_t[:nsel, :nc] = cover.T
    key_blk = np.arange(s) // SEL_BLOCK
    expand = (key_blk[:, None] == np.arange(LANES)[None, :]).astype(np.float32)
    expand = expand.reshape(s // SEL_TILE, SEL_TILE, LANES)
    return jnp.asarray(cov_t, BF16), jnp.asarray(expand, BF16)


def _nsa_attention(proj_a, proj_b, cmp_kv, bias, bn, s):
    t = proj_a.shape[0]
    g_n, dh, rep = NSA_KV_GROUPS, NSA_HEAD_DIM, NSA_REP
    gw = rep * dh
    nq = s // Q_BLOCK
    nch = s // CMP_STRIDE
    nsel = s // SEL_BLOCK
    nd = bias.shape[0] - 1
    cov_t, expand = _nsa_layout(s)
    ksel0 = (NSA_WIDTH + 2 * NSA_KV_WIDTH) // dh
    vsel0 = 0
    kwin0 = NSA_KV_WIDTH // dh
    vwin0 = 2 * NSA_KV_WIDTH // dh
    kcol = lambda c0: pl.BlockSpec((s, dh), lambda b, g, i: (b, c0 + g))
    gp0 = 3 * NSA_KV_WIDTH // gw
    bg0 = (3 * NSA_KV_WIDTH + NSA_WIDTH) // LANES
    kern = functools.partial(_nsa_kernel, nsel=nsel, topk=min(SEL_TOPK, nsel), nd=nd)
    steps = nq // NSA_QBLOCKS
    rows = NSA_QBLOCKS * Q_BLOCK
    return pl.pallas_call(
        kern,
        grid=(bn, g_n, steps),
        in_specs=[
            pl.BlockSpec((rows, gw), lambda b, g, i: (b * steps + i, g)),
            pl.BlockSpec((None, nch, dh), lambda b, g, i: (0, b * g_n + g, 0)),
            pl.BlockSpec((None, nch, dh), lambda b, g, i: (1, b * g_n + g, 0)),
            kcol(ksel0), kcol(vsel0), kcol(kwin0), kcol(vwin0),
            pl.BlockSpec((rows, gw), lambda b, g, i: (b * steps + i, gp0 + g)),
            pl.BlockSpec((rows, LANES), lambda b, g, i: (b * steps + i, bg0 + g)),
            pl.BlockSpec((nd + 1, None, KEY_TILE, rep * Q_BLOCK), lambda b, g, i: (0, g, 0, 0),
                         pipeline_mode=pl.Buffered(1)),
            _resident(cov_t.shape, lambda b, g, i: (0, 0)),
            _resident(expand.shape, lambda b, g, i: (0, 0, 0)),
        ],
        out_specs=pl.BlockSpec((rows, gw), lambda b, g, i: (b * steps + i, g)),
        out_shape=jax.ShapeDtypeStruct((t, NSA_WIDTH), BF16),
        scratch_shapes=[pltpu.VMEM((dh, NSA_QBLOCKS * rep * Q_BLOCK), F32)],
        compiler_params=_params("parallel", "parallel", "arbitrary"),
        name="nsa_attention",
    )(proj_a, cmp_kv, cmp_kv, proj_a, proj_b, proj_b, proj_b, proj_b, proj_b, bias, cov_t, expand)


def _nsa_in_weights(w_in):
    split = NSA_WIDTH + 3 * NSA_KV_WIDTH
    main = NSA_WIDTH + 6 * NSA_KV_WIDTH
    ngate = 3 * NSA_HEADS
    per_group = ngate // NSA_KV_GROUPS
    d = w_in.shape[0]
    bg = w_in[:, main:main + ngate].reshape(d, NSA_KV_GROUPS, per_group)
    bg = jnp.pad(bg, ((0, 0), (0, 0), (0, LANES - per_group))).reshape(d, NSA_KV_GROUPS * LANES)
    w_b = jnp.concatenate([w_in[:, split:main], w_in[:, main + ngate:], bg], axis=1)
    return w_in[:, :split].astype(BF16), w_b.astype(BF16)


def _nsa_mixer(h, g_pre, bn, s, w_in, pos_k, w1_k, w2_k, pos_v, w1_v, w2_v, rel_bias):
    g_n, dh = NSA_KV_GROUPS, NSA_HEAD_DIM
    w_a, w_b = _nsa_in_weights(w_in)
    proj_a = _norm_matmul(h, g_pre, w_a)
    proj_b = _norm_matmul(h, g_pre, w_b)
    nch = s // CMP_STRIDE
    p3 = proj_a.reshape(bn, nch, CMP_STRIDE, proj_a.shape[1])

    def chunks(c0):
        x = p3[..., c0:c0 + NSA_KV_WIDTH].reshape(bn, nch, CMP_STRIDE, g_n, dh)
        return jnp.transpose(x, (0, 3, 1, 2, 4)).reshape(bn * g_n * nch, CMP_STRIDE * dh)

    xkv = jnp.stack([chunks(NSA_WIDTH), chunks(NSA_WIDTH + NSA_KV_WIDTH)])
    half = CMP_STRIDE * dh
    pos = jnp.stack([pos_k.reshape(2, half), pos_v.reshape(2, half)])
    w1 = jnp.stack([w1_k.reshape(2, half, -1), w1_v.reshape(2, half, -1)]).astype(BF16)
    w2 = jnp.stack([w2_k, w2_v]).astype(BF16)
    cmp_kv = _compress(xkv, pos, w1, w2, nch)
    nd = min(s // Q_BLOCK, 15)
    bias = _causal_bias(rel_bias, nd)
    return _nsa_attention(proj_a, proj_b, cmp_kv, bias, bn, s)


def kernel(x, p, rel_bias, norm_pre, norm_post, ple_w_proj, ple_w_gate, a_w_in, a_conv_w, a_conv_b,
           a_w_r, a_b_r, a_w_i, a_b_i, a_lam, a_w_out, b_w_in, b_w_out, c_w_in, c_cmp_pos_k,
           c_cmp_w1_k, c_cmp_w2_k, c_cmp_pos_v, c_cmp_w1_v, c_cmp_w2_v, c_w_out):
    bn, s, d = x.shape
    depth = p.shape[0]
    t = bn * s
    h = x.reshape(t, d)
    for i in range(depth):
        kind = i % N_MIXERS
        j = i // N_MIXERS
        if kind == 0:
            proj = _norm_matmul(h, norm_pre[i], a_w_in[j].astype(BF16))
            y = _rglru(proj, bn, s, a_conv_w[j], a_conv_b[j], a_w_r[j], a_b_r[j], a_w_i[j],
                       a_b_i[j], a_lam[j])
            w_out = a_w_out[j]
        elif kind == 1:
            y = _dilated_mixer(h, norm_pre[i], bn, s, b_w_in[j], rel_bias)
            w_out = b_w_out[j]
        else:
            y = _nsa_mixer(h, norm_pre[i], bn, s, c_w_in[j], c_cmp_pos_k[j], c_cmp_w1_k[j],
                           c_cmp_w2_k[j], c_cmp_pos_v[j], c_cmp_w1_v[j], c_cmp_w2_v[j], rel_bias)
            w_out = c_w_out[j]
        h = _out_ple(y, w_out.astype(BF16), norm_post[i], h, p[i].reshape(t, -1),
                     ple_w_proj[i].astype(BF16), ple_w_gate[i].astype(BF16))
    return h.reshape(bn, s, d)
```

```python
import functools
import math

import numpy as np
import jax
import jax.numpy as jnp
from jax import lax
from jax.experimental import pallas as pl
from jax.experimental.pallas import tpu as pltpu

F32 = jnp.float32
BF16 = jnp.bfloat16

NORM_EPS = 1e-6
NEG_INF = -1e30
TINY = 1e-30
MASK_BIG = 2.0 ** 40
LOG2_E = math.log2(math.e)

N_MIXERS = 3
N_BUCKETS = 32
BUCKET_MAX_DIST = 2048

LRU_BLOCKS = 8
CONV_W = 4
LRU_C = 8.0

DIL_PATTERNS = ((128, 1), (512, 4), (2048, 16))
DIL_HEADS = 16
DIL_HEAD_DIM = 64
DIL_WIDTH = DIL_HEADS * DIL_HEAD_DIM

NSA_HEADS = 16
NSA_KV_GROUPS = 4
NSA_REP = NSA_HEADS // NSA_KV_GROUPS
NSA_HEAD_DIM = 128
NSA_WIDTH = NSA_HEADS * NSA_HEAD_DIM
NSA_KV_WIDTH = NSA_KV_GROUPS * NSA_HEAD_DIM
CMP_BLOCK = 32
CMP_STRIDE = 16
SEL_BLOCK = 64
SEL_TOPK = 16
WIN = 512
FORCE_BONUS = 1e4

SUBLANES = 8
LANES = 128
Q_BLOCK = 128
KEY_TILE = 256
DIL_WINDOW = Q_BLOCK * max(d for _, d in DIL_PATTERNS)
SEL_TILE = 1024
NSA_QBLOCKS = 8
PROJ_ROWS = 512
PROJ_HALVES = 2
WIN_SPAN = WIN + Q_BLOCK
VMEM_LIMIT_BYTES = 56 * 1024 * 1024


def _params(*sem):
    return pltpu.CompilerParams(dimension_semantics=sem, vmem_limit_bytes=VMEM_LIMIT_BYTES)


def _dot(a, b):
    return jnp.dot(a, b, preferred_element_type=F32)


def _dot_nt(a, b):
    return lax.dot_general(a, b, (((1,), (1,)), ((), ())), preferred_element_type=F32)


def _dot_tn(a, b):
    return lax.dot_general(a, b, (((0,), (0,)), ((), ())), preferred_element_type=F32)


def _sigmoid(x):
    return 1.0 / (1.0 + jnp.exp(-x))


def _silu(x):
    return x * _sigmoid(x)


def _resident(block_shape, index_map):
    return pl.BlockSpec(block_shape, index_map, pipeline_mode=pl.Buffered(1))


def _norm_matmul_kernel(h_ref, g_ref, w_ref, o_ref, *scratch, dil):
    n = w_ref.shape[1]
    half_rows = h_ref.shape[0] // PROJ_HALVES
    for hh in range(PROJ_HALVES):
        x = h_ref[hh * half_rows:(hh + 1) * half_rows, :]
        ms = jnp.mean(x * x, axis=-1, keepdims=True)
        u = (x * lax.rsqrt(ms + NORM_EPS) * g_ref[...]).astype(BF16)
        res = _dot(u, w_ref[...])
        if dil == 1:
            o_ref[hh * half_rows:(hh + 1) * half_rows, :] = res.astype(o_ref.dtype)
            continue
        res_scr = scratch[0]
        per = half_rows // dil
        for c in range(n // LANES):
            res_scr[hh, c] = res[:, c * LANES:(c + 1) * LANES]
        for r in range(dil):
            for c in range(n // LANES):
                o_ref[r, hh * per:(hh + 1) * per, c * LANES:(c + 1) * LANES] = (
                    res_scr[hh, c, pl.ds(r, per, stride=dil), :].astype(o_ref.dtype))


def _norm_matmul(h, g, w, dil=1, bn=1):
    t, d = h.shape
    n = w.shape[1]
    tm = min(t, PROJ_ROWS)
    in_specs = [
        pl.BlockSpec((tm, d), lambda i: (i, 0)),
        _resident((1, d), lambda i: (0, 0)),
        _resident((d, n), lambda i: (0, 0)),
    ]
    if dil > 1:
        s = t // bn
        nt = s // tm
        out_spec = pl.BlockSpec((None, dil, tm // dil, n), lambda i: (i // nt, 0, i % nt, 0))
        out_shape = jax.ShapeDtypeStruct((bn, dil, s // dil, n), BF16)
        scratch = [pltpu.VMEM((PROJ_HALVES, n // LANES, tm // PROJ_HALVES, LANES), F32)]
    else:
        out_spec = pl.BlockSpec((tm, n), lambda i: (i, 0))
        out_shape = jax.ShapeDtypeStruct((t, n), BF16)
        scratch = []
    return pl.pallas_call(
        functools.partial(_norm_matmul_kernel, dil=dil),
        grid=(t // tm,),
        in_specs=in_specs,
        out_specs=out_spec,
        out_shape=out_shape,
        scratch_shapes=scratch,
        compiler_params=_params("parallel"),
        name="norm_matmul" if dil == 1 else f"norm_matmul_residue_{dil}",
    )(h, g.reshape(1, d), w)


def _out_ple_kernel(y_ref, wo_ref, g_ref, h_ref, p_ref, wp_ref, wg_ref, o_ref):
    z = _dot(y_ref[...], wo_ref[...])
    ms = jnp.mean(z * z, axis=-1, keepdims=True)
    h1 = h_ref[...] + z * lax.rsqrt(ms + NORM_EPS) * g_ref[...]
    gate = _sigmoid(_dot(h1.astype(BF16), wg_ref[...]))
    pe = _dot(p_ref[...].astype(BF16), wp_ref[...])
    o_ref[...] = h1 + gate * pe


def _out_ple(y, w_out, g_post, h, p, w_proj, w_gate):
    t, d = h.shape
    k = y.shape[1]
    pd = p.shape[1]
    tm = min(t, 256)
    return pl.pallas_call(
        _out_ple_kernel,
        grid=(t // tm,),
        in_specs=[
            pl.BlockSpec((tm, k), lambda i: (i, 0)),
            _resident((k, d), lambda i: (0, 0)),
            _resident((1, d), lambda i: (0, 0)),
            pl.BlockSpec((tm, d), lambda i: (i, 0)),
            pl.BlockSpec((tm, pd), lambda i: (i, 0)),
            _resident((pd, d), lambda i: (0, 0)),
            _resident((d, d), lambda i: (0, 0)),
        ],
        out_specs=pl.BlockSpec((tm, d), lambda i: (i, 0)),
        out_shape=jax.ShapeDtypeStruct((t, d), F32),
        compiler_params=_params("parallel"),
        name="out_ple",
    )(y, w_out, g_post.reshape(1, d), h, p, w_proj, w_gate)


def _lru_kernel(xb_ref, gate_ref, cw_ref, cb_ref, wr_ref, br_ref, wi_ref, bi_ref, lam_ref,
                y_ref, xext_scr, h_scr, a_scr, b_scr, h_scan, *, ts, bw):
    @pl.when(pl.program_id(1) == 0)
    def _():
        xext_scr[0:SUBLANES, :] = jnp.zeros((SUBLANES, xext_scr.shape[1]), F32)
        h_scr[...] = jnp.zeros(h_scr.shape, F32)

    seg = ts // SUBLANES
    pitch = seg + SUBLANES
    row = lax.broadcasted_iota(jnp.int32, (SUBLANES, LANES), 0)
    for n in range(LRU_BLOCKS):
        cols = slice(n * bw, (n + 1) * bw)
        x = xb_ref[:, cols].astype(F32)
        xext_scr[SUBLANES:SUBLANES + ts, cols] = x
        xc = cw_ref[3:4, cols] * x + cb_ref[:, cols]
        for k in range(1, CONV_W):
            xc = xc + cw_ref[3 - k:4 - k, cols] * xext_scr[SUBLANES - k:SUBLANES - k + ts, cols]
        xext_scr[0:SUBLANES, cols] = x[ts - SUBLANES:ts, :]

        xn = xc.astype(BF16)
        r = _sigmoid(_dot(xn, wr_ref[n]) + br_ref[:, cols])
        ig = _sigmoid(_dot(xn, wi_ref[n]) + bi_ref[:, cols])
        nl = -lam_ref[:, cols]
        softplus = jnp.maximum(nl, 0.0) + jnp.log(1.0 + jnp.exp(-jnp.abs(nl)))
        a = jnp.exp2(r * ((-LRU_C * LOG2_E) * softplus))
        b = jnp.sqrt(1.0 - a * a) * (ig * xc)

        for sl in range(bw // LANES):
            lanes = slice(n * bw + sl * LANES, n * bw + (sl + 1) * LANES)
            slab = n * (bw // LANES) + sl
            for sg in range(SUBLANES):
                a_scr[slab, sg * pitch:sg * pitch + seg, :] = a[sg * seg:(sg + 1) * seg, sl * LANES:(sl + 1) * LANES]
                b_scr[slab, sg * pitch:sg * pitch + seg, :] = b[sg * seg:(sg + 1) * seg, sl * LANES:(sl + 1) * LANES]
            h_loc = jnp.zeros((SUBLANES, LANES), F32)
            a_loc = jnp.ones((SUBLANES, LANES), F32)
            h_steps, a_steps = [], []
            for j in range(seg):
                a_j = a_scr[slab, pl.ds(j, SUBLANES, stride=pitch), :]
                b_j = b_scr[slab, pl.ds(j, SUBLANES, stride=pitch), :]
                h_loc = a_j * h_loc + b_j
                a_loc = a_j * a_loc
                h_steps.append(h_loc)
                a_steps.append(a_loc)
            a_tot, h_tot = a_loc, h_loc
            for dist in (1, 2, 4):
                a_sh = jnp.where(row >= dist, pltpu.roll(a_tot, dist, axis=0), 1.0)
                h_sh = jnp.where(row >= dist, pltpu.roll(h_tot, dist, axis=0), 0.0)
                h_tot = a_tot * h_sh + h_tot
                a_tot = a_tot * a_sh
            h_in = h_scr[:, lanes]
            h_end = a_tot * h_in + h_tot
            h_enter = jnp.where(row >= 1, pltpu.roll(h_end, 1, axis=0), h_in)
            for j in range(seg):
                h_scan[slab, pl.ds(j, SUBLANES, stride=pitch), :] = h_steps[j] + a_steps[j] * h_enter
            h_scr[:, lanes] = jnp.broadcast_to(h_end[SUBLANES - 1:SUBLANES, :], (SUBLANES, LANES))
            hs = jnp.concatenate(
                [h_scan[slab, sg * pitch:sg * pitch + seg, :] for sg in range(SUBLANES)], axis=0)
            y_ref[:, lanes] = (hs * _silu(gate_ref[:, lanes].astype(F32))).astype(BF16)


def _rglru(proj, bn, s, conv_w, conv_b, w_r, b_r, w_i, b_i, lam):
    t, two_c = proj.shape
    c = two_c // 2
    bw = c // LRU_BLOCKS
    ts = min(s, 256)
    nst = s // ts
    row = lambda v: v.reshape(1, c)
    kern = functools.partial(_lru_kernel, ts=ts, bw=bw)
    return pl.pallas_call(
        kern,
        grid=(bn, nst),
        in_specs=[
            pl.BlockSpec((ts, c), lambda b, i: (b * nst + i, 0)),
            pl.BlockSpec((ts, c), lambda b, i: (b * nst + i, 1)),
            pl.BlockSpec((CONV_W, c), lambda b, i: (0, 0)),
            pl.BlockSpec((1, c), lambda b, i: (0, 0)),
            pl.BlockSpec((LRU_BLOCKS, bw, bw), lambda b, i: (0, 0, 0)),
            pl.BlockSpec((1, c), lambda b, i: (0, 0)),
            pl.BlockSpec((LRU_BLOCKS, bw, bw), lambda b, i: (0, 0, 0)),
            pl.BlockSpec((1, c), lambda b, i: (0, 0)),
            pl.BlockSpec((1, c), lambda b, i: (0, 0)),
        ],
        out_specs=pl.BlockSpec((ts, c), lambda b, i: (b * nst + i, 0)),
        out_shape=jax.ShapeDtypeStruct((t, c), BF16),
        scratch_shapes=[pltpu.VMEM((ts + SUBLANES, c), F32), pltpu.VMEM((SUBLANES, c), F32)]
        + [pltpu.VMEM((c // LANES, ts + SUBLANES * SUBLANES, LANES), F32)] * 3,
        compiler_params=_params("parallel", "arbitrary"),
        name="rglru",
    )(proj, proj, conv_w, row(conv_b), w_r.astype(BF16), row(b_r), w_i.astype(BF16), row(b_i),
      row(lam))


def _t5_bucket_np(dist):
    n = np.maximum(dist, 0)
    max_exact = N_BUCKETS // 2
    nf = np.maximum(n, max_exact).astype(np.float64)
    large = max_exact + (np.log(nf / max_exact) / math.log(BUCKET_MAX_DIST / max_exact)
                         * (N_BUCKETS - max_exact)).astype(np.int64)
    return np.where(n < max_exact, n, np.minimum(large, N_BUCKETS - 1)).astype(np.int32)


TOEPLITZ_PERIOD = Q_BLOCK + KEY_TILE


def _toeplitz_kernel(w_ref, o_ref, *, rows, cols, side_by_side, scale):
    for h in range(w_ref.shape[0]):
        gen = jnp.broadcast_to(w_ref[h:h + 1, :] * scale, (rows, w_ref.shape[1]))
        tile = pltpu.roll(gen, 0, 1, stride=1, stride_axis=0)[:, :cols]
        o_ref[h // side_by_side, :, (h % side_by_side) * cols:(h % side_by_side + 1) * cols] = tile


def _toeplitz(w, rows, cols, side_by_side=1, scale=1.0):
    n, heads, period = w.shape
    out_block = (None, heads // side_by_side, rows, side_by_side * cols)
    return pl.pallas_call(
        functools.partial(_toeplitz_kernel, rows=rows, cols=cols, side_by_side=side_by_side, scale=scale),
        grid=(n,),
        in_specs=[pl.BlockSpec((None, heads, period), lambda i: (i, 0, 0))],
        out_specs=pl.BlockSpec(out_block, lambda i: (i, 0, 0, 0)),
        out_shape=jax.ShapeDtypeStruct((n,) + out_block[1:], F32),
        compiler_params=_params("parallel"),
        name="toeplitz_bias",
    )(w)


def _signed_offsets(n_cols):
    m = np.arange(TOEPLITZ_PERIOD)
    return np.where(m < n_cols, m, m - TOEPLITZ_PERIOD)


def _dilated_bias(rel_bias):
    delta = Q_BLOCK - _signed_offsets(2 * Q_BLOCK)
    gens = []
    for window, dil in DIL_PATTERNS:
        ok = (delta >= 0) & (delta <= window // dil)
        gens.append(jnp.where(jnp.asarray(ok)[:, None], rel_bias[_t5_bucket_np(delta * dil)], NEG_INF))
    tbl = _toeplitz(jnp.transpose(jnp.stack(gens), (0, 2, 1)).astype(F32), Q_BLOCK, 2 * Q_BLOCK,
                    scale=LOG2_E)
    return [tbl[g] for g in range(len(DIL_PATTERNS))]


def _causal_bias(rel_bias, n_dist):
    heads = rel_bias.shape[1]
    dist = np.arange(n_dist)[:, None] * Q_BLOCK + _signed_offsets(Q_BLOCK)[None, :]
    vec = jnp.where(jnp.asarray(dist >= 0)[..., None], rel_bias[_t5_bucket_np(dist)], NEG_INF)
    vec = jnp.transpose(vec, (0, 2, 1)).astype(F32)
    vec = jnp.concatenate([vec, jnp.full((1, heads, TOEPLITZ_PERIOD), NEG_INF, F32)])
    return _toeplitz(vec, KEY_TILE, Q_BLOCK, side_by_side=NSA_REP, scale=LOG2_E)


def _dil_kernel(*refs, dils):
    ng = len(dils)
    groups = [refs[5 * g:5 * g + 5] for g in range(ng)]
    bias_refs = refs[5 * ng:6 * ng]
    gate_ref, y_ref = refs[6 * ng], refs[6 * ng + 1]
    ext = refs[6 * ng + 2:8 * ng + 2]
    o_acc, lse_acc = refs[8 * ng + 2:]
    first = pl.program_id(1) == 0
    scale = DIL_HEAD_DIM ** -0.5 * LOG2_E
    lane = lax.broadcasted_iota(jnp.int32, (Q_BLOCK, LANES), 1)
    low = lane < DIL_HEAD_DIM
    col = lax.broadcasted_iota(jnp.int32, (Q_BLOCK, 2 * Q_BLOCK), 1)
    halo_pen = jnp.where(col < Q_BLOCK, jnp.where(first, NEG_INF, 0.0), 0.0)
    for g, dil in enumerate(dils):
        q_ref, k_ref, kh_ref, v_ref, vh_ref = groups[g]
        kext, vext = ext[2 * g], ext[2 * g + 1]
        kext[:, 0:Q_BLOCK, :] = kh_ref[...]
        kext[:, Q_BLOCK:, :] = k_ref[...]
        vext[:, 0:Q_BLOCK, :] = vh_ref[...]
        vext[:, Q_BLOCK:, :] = v_ref[...]
        for r in range(dil):
            for j in range(k_ref.shape[1] // Q_BLOCK):
                qp = q_ref[r, j * Q_BLOCK:(j + 1) * Q_BLOCK, :]
                kb = kext[r, j * Q_BLOCK:(j + 2) * Q_BLOCK, :]
                vb = vext[r, j * Q_BLOCK:(j + 2) * Q_BLOCK, :]
                o_half, lse_half = [], []
                for half in range(2):
                    qm = jnp.where(low if half == 0 else jnp.logical_not(low), qp, jnp.zeros_like(qp))
                    sc = _dot_nt(qm, kb) * scale + bias_refs[g][half]
                    if j == 0:
                        sc = sc + halo_pen
                    m = jnp.max(sc, axis=-1, keepdims=True)
                    e = jnp.exp2(sc - m)
                    l = jnp.sum(e, axis=-1, keepdims=True)
                    pv = _dot(e.astype(BF16), vb)
                    o_half.append(pv / l)
                    lse_half.append(jnp.broadcast_to(m + jnp.log2(l), (Q_BLOCK, LANES)))
                o_new = jnp.where(low, o_half[0], o_half[1])
                lse_new = jnp.where(low, lse_half[0], lse_half[1])
                if dil == 1:
                    rows = pl.ds(j * Q_BLOCK, Q_BLOCK)
                else:
                    rows = pl.ds(r + dil * j * Q_BLOCK, Q_BLOCK, stride=dil)
                if g == 0:
                    o_acc[rows, :] = o_new
                    lse_acc[rows, :] = lse_new
                else:
                    o_old = o_acc[rows, :]
                    lse_old = lse_acc[rows, :]
                    top = jnp.maximum(lse_old, lse_new)
                    w_old = jnp.exp2(lse_old - top)
                    w_new = jnp.exp2(lse_new - top)
                    tot = w_old + w_new
                    o_acc[rows, :] = (w_old * o_old + w_new * o_new) / tot
                    if g + 1 < ng:
                        lse_acc[rows, :] = top + jnp.log2(tot)
    y_ref[...] = (o_acc[...] * _silu(gate_ref[...].astype(F32))).astype(BF16)


def _dil_attention(arrays, biases, bn, s):
    dils = tuple(d for _, d in DIL_PATTERNS)
    win = DIL_WINDOW
    pairs = DIL_WIDTH // LANES
    nw = s // win
    in_specs, operands, scratch = [], [], []
    for g, dil in enumerate(dils):
        per = win // dil
        per_blocks = per // Q_BLOCK
        cur = lambda cb, dil=dil, per=per: pl.BlockSpec(
            (None, dil, per, LANES), lambda b, w, p: (b, 0, w, cb * pairs + p))
        halo = lambda cb, dil=dil, per_blocks=per_blocks: pl.BlockSpec(
            (None, dil, Q_BLOCK, LANES),
            lambda b, w, p: (b, 0, jnp.maximum(w * per_blocks - 1, 0), cb * pairs + p))
        in_specs += [cur(0), cur(1), halo(1), cur(2), halo(2)]
        operands += [arrays[g]] * 5
        scratch += [pltpu.VMEM((dil, per + Q_BLOCK, LANES), BF16)] * 2
    in_specs += [pl.BlockSpec((2, Q_BLOCK, 2 * Q_BLOCK), lambda b, w, p: (p, 0, 0))] * len(dils)
    operands += list(biases)
    in_specs.append(pl.BlockSpec((None, None, win, LANES), lambda b, w, p: (b, 0, w, 3 * pairs + p)))
    operands.append(arrays[0])
    scratch += [pltpu.VMEM((win, LANES), F32)] * 2
    return pl.pallas_call(
        functools.partial(_dil_kernel, dils=dils),
        grid=(bn, nw, pairs),
        in_specs=in_specs,
        out_specs=pl.BlockSpec((win, LANES), lambda b, w, p: (b * nw + w, p)),
        out_shape=jax.ShapeDtypeStruct((bn * s, DIL_WIDTH), BF16),
        scratch_shapes=scratch,
        compiler_params=_params("parallel", "arbitrary", "arbitrary"),
        name="dilated_attention",
    )(*operands)


def _dilated_mixer(h, g_pre, bn, s, w_in, rel_bias):
    w = DIL_WIDTH
    ng = len(DIL_PATTERNS)
    wb = w_in.astype(BF16)
    main_w = jnp.concatenate([wb[:, :3 * w], wb[:, 3 * ng * w:]], axis=1)
    arrays = [_norm_matmul(h, g_pre, main_w).reshape(bn, 1, s, 4 * w)]
    for gi in range(1, ng):
        dil = DIL_PATTERNS[gi][1]
        arrays.append(_norm_matmul(h, g_pre, wb[:, 3 * gi * w:3 * (gi + 1) * w], dil=dil, bn=bn))
    return _dil_attention(arrays, _dilated_bias(rel_bias), bn, s)


def _gelu_tanh(x):
    return 0.5 * x * (1.0 + jnp.tanh(math.sqrt(2.0 / math.pi) * (x + 0.044715 * (x * x * x))))


def _compress_kernel(x_ref, pos_ref, w1_ref, w2_ref, o_ref, *, nch):
    x = x_ref[...].astype(F32)
    top = _dot((x + pos_ref[0:1, :]).astype(BF16), w1_ref[0])
    bot = _dot((x + pos_ref[1:2, :]).astype(BF16), w1_ref[1])
    hid = top + pltpu.roll(bot, nch - 1, axis=0)
    o_ref[...] = _dot(_gelu_tanh(hid).astype(BF16), w2_ref[...]).astype(BF16)


def _compress(xkv, pos, w1, w2, nch):
    _, rows, width = xkv.shape
    hidden = w1.shape[-1]
    dh = w2.shape[-1]
    return pl.pallas_call(
        functools.partial(_compress_kernel, nch=nch),
        grid=(2, rows // nch),
        in_specs=[
            pl.BlockSpec((None, nch, width), lambda kv, i: (kv, i, 0)),
            pl.BlockSpec((None, 2, width), lambda kv, i: (kv, 0, 0)),
            pl.BlockSpec((None, 2, width, hidden), lambda kv, i: (kv, 0, 0, 0)),
            pl.BlockSpec((None, hidden, dh), lambda kv, i: (kv, 0, 0)),
        ],
        out_specs=pl.BlockSpec((None, nch, dh), lambda kv, i: (kv, i, 0)),
        out_shape=jax.ShapeDtypeStruct((2, rows, dh), BF16),
        compiler_params=_params("parallel", "parallel"),
        name="nsa_compress",
    )(xkv, pos, w1, w2)


def _bias_tile(bias_ref, d, nd):
    return bias_ref[jnp.where(d < 0, nd, jnp.minimum(d, nd - 1))]


def _nsa_front(i, q_blk, kc_ref, vc_ref, kw_ref, vw_ref, bias_ref, cov_ref, *, nsel, topk, nd):
    t0 = i * Q_BLOCK
    dh = NSA_HEAD_DIM
    rq = NSA_REP * Q_BLOCK
    scale = dh ** -0.5 * LOG2_E
    q_all = jnp.concatenate([q_blk[:, r * dh:(r + 1) * dh] for r in range(NSA_REP)], axis=0)
    qs = (q_all.astype(F32) * scale).astype(BF16)

    def bias_tile(d):
        return _bias_tile(bias_ref, d, nd)

    def per_head(x):
        return jnp.concatenate([x] * NSA_REP, axis=1)

    kc = kc_ref[...]
    nch = kc.shape[0]
    t_q = t0 + (lax.broadcasted_iota(jnp.int32, (nch, rq), 1) & (Q_BLOCK - 1))
    cend = lax.broadcasted_iota(jnp.int32, (nch, rq), 0) * CMP_STRIDE + (CMP_BLOCK - 1)
    cmask = cend <= t_q
    sc = jnp.where(cmask, _dot_nt(kc, qs), NEG_INF)
    m = jnp.max(sc, axis=0, keepdims=True)
    e = jnp.where(cmask, jnp.exp2(sc - m), 0.0)
    prob = e / jnp.maximum(jnp.sum(e, axis=0, keepdims=True), TINY)
    o_cmp = _dot_tn(vc_ref[...], prob.astype(BF16))
    psum = prob[:, 0:Q_BLOCK]
    for r in range(1, NSA_REP):
        psum = psum + prob[:, r * Q_BLOCK:(r + 1) * Q_BLOCK]
    p_hi = psum.astype(BF16)
    p_lo = (psum - p_hi.astype(F32)).astype(BF16)
    cov = cov_ref[...]
    imp = (_dot(cov, p_hi) + _dot(cov, p_lo))[0:nsel, :]
    blk = lax.broadcasted_iota(jnp.int32, (nsel, Q_BLOCK), 0)
    t_col = t0 + lax.broadcasted_iota(jnp.int32, (nsel, Q_BLOCK), 1)
    cur = t_col // SEL_BLOCK
    forced = (blk == 0) | (blk == cur) | (blk == cur - 1)
    valid = blk * SEL_BLOCK <= t_col
    imp = jnp.where(valid, imp + jnp.where(forced, FORCE_BONUS, 0.0), NEG_INF)
    ngrp = nsel // SUBLANES
    rank = [jnp.zeros((SUBLANES, Q_BLOCK), jnp.int32) for _ in range(ngrp)]
    parts = [imp[g * SUBLANES:(g + 1) * SUBLANES, :] for g in range(ngrp)]
    sub = lax.broadcasted_iota(jnp.int32, (SUBLANES, Q_BLOCK), 0)
    for mrow in range(nsel):
        other = jnp.broadcast_to(imp[mrow:mrow + 1, :], (SUBLANES, Q_BLOCK))
        for g in range(ngrp):
            ge = jnp.where(other >= parts[g], 1, 0)
            gt = jnp.where(other > parts[g], 1, 0)
            if g * SUBLANES > mrow:
                before = ge
            elif (g + 1) * SUBLANES - 1 < mrow:
                before = gt
            else:
                before = jnp.where(sub + g * SUBLANES > mrow, ge, gt)
            rank[g] = rank[g] + before
    rank = jnp.concatenate(rank, axis=0)
    sel_t = jnp.where(valid & (rank < topk), 1.0, 0.0)
    if nsel < LANES:
        sel_t = jnp.concatenate([sel_t, jnp.zeros((LANES - nsel, Q_BLOCK), F32)], axis=0)

    j0 = jnp.maximum(i - WIN // Q_BLOCK, 0)
    start = pl.multiple_of(j0 * Q_BLOCK, Q_BLOCK)
    dbase = i - j0
    sc = _dot_nt(kw_ref[pl.ds(start, WIN_SPAN), :], qs)
    dist = (dbase * Q_BLOCK + lax.broadcasted_iota(jnp.int32, (Q_BLOCK, Q_BLOCK), 1)
            - lax.broadcasted_iota(jnp.int32, (Q_BLOCK, Q_BLOCK), 0))
    too_old = per_head(jnp.where(dist <= WIN - 1, 0.0, NEG_INF))
    n_full, rem = divmod(WIN_SPAN, KEY_TILE)
    pieces = [bias_tile(dbase - (KEY_TILE // Q_BLOCK) * u) for u in range(n_full)]
    if rem:
        pieces.append(bias_tile(dbase - (KEY_TILE // Q_BLOCK) * n_full)[0:rem])
    pieces[0] = jnp.concatenate([pieces[0][0:Q_BLOCK] + too_old, pieces[0][Q_BLOCK:]], axis=0)
    sc = sc + jnp.concatenate(pieces, axis=0)
    m = jnp.max(sc, axis=0, keepdims=True)
    e = jnp.exp2(sc - m)
    l_win = jnp.sum(e, axis=0, keepdims=True)
    o_win = _dot_tn(vw_ref[pl.ds(start, WIN_SPAN), :], e.astype(BF16)) / jnp.maximum(l_win, TINY)

    sel_pen = jnp.transpose(jnp.where(sel_t > 0.5, 0.0, -MASK_BIG)).astype(BF16)
    return qs, o_cmp, o_win, sel_pen


def _nsa_kernel(q_ref, kc_ref, vc_ref, ks_ref, vs_ref, kw_ref, vw_ref, gp_ref, bg_ref,
                bias_ref, cov_ref, exp_ref, y_ref, acc_scr, *, nsel, topk, nd):
    step = pl.program_id(2)
    rq = NSA_REP * Q_BLOCK
    cols = NSA_QBLOCKS * rq
    blocks = [step * NSA_QBLOCKS + qb for qb in range(NSA_QBLOCKS)]
    fronts = [
        _nsa_front(i, q_ref[qb * Q_BLOCK:(qb + 1) * Q_BLOCK, :], kc_ref, vc_ref, kw_ref, vw_ref,
                   bias_ref, cov_ref, nsel=nsel, topk=topk, nd=nd)
        for qb, i in enumerate(blocks)]

    sub_tiles = SEL_TILE // Q_BLOCK
    q_aug = jnp.concatenate(
        [jnp.concatenate([qs, jnp.concatenate([sel_pen] * NSA_REP, axis=0)], axis=1)
         for qs, _, _, sel_pen in fronts], axis=0)
    acc_scr[...] = jnp.zeros(acc_scr.shape, F32)

    def sel_body(kt, carry):
        m_prev, l_prev = carry
        k0 = pl.multiple_of(kt * SEL_TILE, SEL_TILE)
        k_aug = jnp.concatenate([ks_ref[pl.ds(k0, SEL_TILE), :], exp_ref[kt]], axis=1)
        bias = jnp.concatenate(
            [jnp.concatenate(
                [_bias_tile(bias_ref, i - sub_tiles * kt - (KEY_TILE // Q_BLOCK) * u, nd) for i in blocks],
                axis=1)
             for u in range(SEL_TILE // KEY_TILE)], axis=0)
        sc = _dot_nt(k_aug, q_aug) + bias
        m_new = jnp.maximum(m_prev, jnp.max(sc, axis=0, keepdims=True))
        alpha = jnp.exp2(m_prev - m_new)
        e = jnp.exp2(sc - m_new)
        l_new = alpha * l_prev + jnp.sum(e, axis=0, keepdims=True)
        acc_scr[...] = alpha * acc_scr[...] + _dot_tn(vs_ref[pl.ds(k0, SEL_TILE), :], e.astype(BF16))
        return m_new, l_new

    init = (jnp.full((1, cols), NEG_INF, F32), jnp.zeros((1, cols), F32))
    _, l_sel = lax.fori_loop(0, blocks[-1] // sub_tiles + 1, sel_body, init)
    o_sel = acc_scr[...] / jnp.maximum(l_sel, TINY)
    for qb, (_, o_cmp, o_win, _) in enumerate(fronts):
        rows = slice(qb * Q_BLOCK, (qb + 1) * Q_BLOCK)
        _nsa_back(o_cmp, o_sel[:, qb * rq:(qb + 1) * rq], o_win, bg_ref[rows, :], gp_ref, y_ref, rows)


def _nsa_back(o_cmp, o_sel, o_win, branch_gates, gp_ref, y_ref, rows):
    dh = NSA_HEAD_DIM
    gates_t = jnp.transpose(_sigmoid(branch_gates.astype(F32)))

    def gate_row(branch):
        return jnp.concatenate(
            [gates_t[3 * r + branch:3 * r + branch + 1, :] for r in range(NSA_REP)], axis=1)

    o_t = gate_row(0) * o_cmp + gate_row(1) * o_sel + gate_row(2) * o_win
    for r in range(NSA_REP):
        o = jnp.transpose(o_t[:, r * Q_BLOCK:(r + 1) * Q_BLOCK])
        gp = gp_ref[rows, r * dh:(r + 1) * dh].astype(F32)
        y_ref[rows, r * dh:(r + 1) * dh] = (o * _silu(gp)).astype(BF16)


def _nsa_layout(s):
    nch = s // CMP_STRIDE
    nc = (s - CMP_BLOCK) // CMP_STRIDE + 1
    nsel = s // SEL_BLOCK
    start = np.arange(nc) * CMP_STRIDE
    sel_start = np.arange(nsel) * SEL_BLOCK
    cover = (start[:, None] < sel_start[None, :] + SEL_BLOCK) & (
        start[:, None] + CMP_BLOCK - 1 >= sel_start[None, :])
    cov_t = np.zeros((LANES, nch), np.float32)
    cov_t[:nsel, :nc] = cover.T
    key_blk = np.arange(s) // SEL_BLOCK
    expand = (key_blk[:, None] == np.arange(LANES)[None, :]).astype(np.float32)
    expand = expand.reshape(s // SEL_TILE, SEL_TILE, LANES)
    return jnp.asarray(cov_t, BF16), jnp.asarray(expand, BF16)


def _nsa_attention(proj_a, proj_b, cmp_kv, bias, bn, s):
    t = proj_a.shape[0]
    g_n, dh, rep = NSA_KV_GROUPS, NSA_HEAD_DIM, NSA_REP
    gw = rep * dh
    nq = s // Q_BLOCK
    nch = s // CMP_STRIDE
    nsel = s // SEL_BLOCK
    nd = bias.shape[0] - 1
    cov_t, expand = _nsa_layout(s)
    ksel0 = (NSA_WIDTH + 2 * NSA_KV_WIDTH) // dh
    vsel0 = 0
    kwin0 = NSA_KV_WIDTH // dh
    vwin0 = 2 * NSA_KV_WIDTH // dh
    kcol = lambda c0: pl.BlockSpec((s, dh), lambda b, g, i: (b, c0 + g))
    gp0 = 3 * NSA_KV_WIDTH // gw
    bg0 = (3 * NSA_KV_WIDTH + NSA_WIDTH) // LANES
    kern = functools.partial(_nsa_kernel, nsel=nsel, topk=min(SEL_TOPK, nsel), nd=nd)
    steps = nq // NSA_QBLOCKS
    rows = NSA_QBLOCKS * Q_BLOCK
    return pl.pallas_call(
        kern,
        grid=(bn, g_n, steps),
        in_specs=[
            pl.BlockSpec((rows, gw), lambda b, g, i: (b * steps + i, g)),
            pl.BlockSpec((None, nch, dh), lambda b, g, i: (0, b * g_n + g, 0)),
            pl.BlockSpec((None, nch, dh), lambda b, g, i: (1, b * g_n + g, 0)),
            kcol(ksel0), kcol(vsel0), kcol(kwin0), kcol(vwin0),
            pl.BlockSpec((rows, gw), lambda b, g, i: (b * steps + i, gp0 + g)),
            pl.BlockSpec((rows, LANES), lambda b, g, i: (b * steps + i, bg0 + g)),
            pl.BlockSpec((nd + 1, None, KEY_TILE, rep * Q_BLOCK), lambda b, g, i: (0, g, 0, 0),
                         pipeline_mode=pl.Buffered(1)),
            _resident(cov_t.shape, lambda b, g, i: (0, 0)),
            _resident(expand.shape, lambda b, g, i: (0, 0, 0)),
        ],
        out_specs=pl.BlockSpec((rows, gw), lambda b, g, i: (b * steps + i, g)),
        out_shape=jax.ShapeDtypeStruct((t, NSA_WIDTH), BF16),
        scratch_shapes=[pltpu.VMEM((dh, NSA_QBLOCKS * rep * Q_BLOCK), F32)],
        compiler_params=_params("parallel", "parallel", "arbitrary"),
        name="nsa_attention",
    )(proj_a, cmp_kv, cmp_kv, proj_a, proj_b, proj_b, proj_b, proj_b, proj_b, bias, cov_t, expand)


def _nsa_in_weights(w_in):
    split = NSA_WIDTH + 3 * NSA_KV_WIDTH
    main = NSA_WIDTH + 6 * NSA_KV_WIDTH
    ngate = 3 * NSA_HEADS
    per_group = ngate // NSA_KV_GROUPS
    d = w_in.shape[0]
    bg = w_in[:, main:main + ngate].reshape(d, NSA_KV_GROUPS, per_group)
    bg = jnp.pad(bg, ((0, 0), (0, 0), (0, LANES - per_group))).reshape(d, NSA_KV_GROUPS * LANES)
    w_b = jnp.concatenate([w_in[:, split:main], w_in[:, main + ngate:], bg], axis=1)
    return w_in[:, :split].astype(BF16), w_b.astype(BF16)


def _nsa_mixer(h, g_pre, bn, s, w_in, pos_k, w1_k, w2_k, pos_v, w1_v, w2_v, rel_bias):
    g_n, dh = NSA_KV_GROUPS, NSA_HEAD_DIM
    w_a, w_b = _nsa_in_weights(w_in)
    proj_a = _norm_matmul(h, g_pre, w_a)
    proj_b = _norm_matmul(h, g_pre, w_b)
    nch = s // CMP_STRIDE
    p3 = proj_a.reshape(bn, nch, CMP_STRIDE, proj_a.shape[1])

    def chunks(c0):
        x = p3[..., c0:c0 + NSA_KV_WIDTH].reshape(bn, nch, CMP_STRIDE, g_n, dh)
        return jnp.transpose(x, (0, 3, 1, 2, 4)).reshape(bn * g_n * nch, CMP_STRIDE * dh)

    xkv = jnp.stack([chunks(NSA_WIDTH), chunks(NSA_WIDTH + NSA_KV_WIDTH)])
    half = CMP_STRIDE * dh
    pos = jnp.stack([pos_k.reshape(2, half), pos_v.reshape(2, half)])
    w1 = jnp.stack([w1_k.reshape(2, half, -1), w1_v.reshape(2, half, -1)]).astype(BF16)
    w2 = jnp.stack([w2_k, w2_v]).astype(BF16)
    cmp_kv = _compress(xkv, pos, w1, w2, nch)
    nd = min(s // Q_BLOCK, 15)
    bias = _causal_bias(rel_bias, nd)
    return _nsa_attention(proj_a, proj_b, cmp_kv, bias, bn, s)


def kernel(x, p, rel_bias, norm_pre, norm_post, ple_w_proj, ple_w_gate, a_w_in, a_conv_w, a_conv_b,
           a_w_r, a_b_r, a_w_i, a_b_i, a_lam, a_w_out, b_w_in, b_w_out, c_w_in, c_cmp_pos_k,
           c_cmp_w1_k, c_cmp_w2_k, c_cmp_pos_v, c_cmp_w1_v, c_cmp_w2_v, c_w_out):
    bn, s, d = x.shape
    depth = p.shape[0]
    t = bn * s
    h = x.reshape(t, d)
    for i in range(depth):
        kind = i % N_MIXERS
        j = i // N_MIXERS
        if kind == 0:
            proj = _norm_matmul(h, norm_pre[i], a_w_in[j].astype(BF16))
            y = _rglru(proj, bn, s, a_conv_w[j], a_conv_b[j], a_w_r[j], a_b_r[j], a_w_i[j],
                       a_b_i[j], a_lam[j])
            w_out = a_w_out[j]
        elif kind == 1:
            y = _dilated_mixer(h, norm_pre[i], bn, s, b_w_in[j], rel_bias)
            w_out = b_w_out[j]
        else:
            y = _nsa_mixer(h, norm_pre[i], bn, s, c_w_in[j], c_cmp_pos_k[j], c_cmp_w1_k[j],
                           c_cmp_w2_k[j], c_cmp_pos_v[j], c_cmp_w1_v[j], c_cmp_w2_v[j], rel_bias)
            w_out = c_w_out[j]
        h = _out_ple(y, w_out.astype(BF16), norm_post[i], h, p[i].reshape(t, -1),
                     ple_w_proj[i].astype(BF16), ple_w_gate[i].astype(BF16))
    return h.reshape(bn, s, d)
```
